```python
import jax, jax.numpy as jnp
from jax import lax
import numpy as np

D_MODEL = 1024
BATCH = 8
SEQ = 8192
DEPTH = 2
DEC_BATCH = 8
DEC_SEQ = 2048
PAST_LEN = 128

RET_HEADS = 6
RET_HEAD_DIM = 64
RET_W = RET_HEADS * RET_HEAD_DIM
RET_CHUNK = 128
MLA_HEADS = 6
MLA_NOPE = 64
MLA_ROPE = 32
MLA_V = 64
MLA_Q_LORA = 384
MLA_KV_LORA = 128
MLA_W = MLA_HEADS * MLA_V
Q_BLOCK = 128
CONV_CH = D_MODEL - RET_W - MLA_W
CONV_K = 31
D_FF = 2816
FFN_CONV_K = 3
ROPE_BASE = 10000.0
EPS = 1e-6
IN_COLS = 4 * RET_W + MLA_Q_LORA + MLA_KV_LORA + MLA_ROPE + 2 * CONV_CH

kernel_name = "hymba_style_retention_mla_conformer_encoder"


def rmsnorm(x, w):
    xf = x.astype(jnp.float32)
    y = xf * lax.rsqrt(jnp.mean(xf * xf, axis=-1, keepdims=True) + EPS)
    return (y * w.astype(jnp.float32)).astype(x.dtype)


def layernorm(x, w, b):
    xf = x.astype(jnp.float32)
    mu = jnp.mean(xf, axis=-1, keepdims=True)
    var = jnp.mean(jnp.square(xf - mu), axis=-1, keepdims=True)
    y = (xf - mu) * lax.rsqrt(var + EPS)
    return (y * w.astype(jnp.float32) + b.astype(jnp.float32)).astype(x.dtype)


def rope(x, pos):
    d = x.shape[-1]
    half = d // 2
    freqs = 1.0 / (ROPE_BASE ** (jnp.arange(half, dtype=jnp.float32) / half))
    ang = pos[:, None] * freqs[None, :]
    cos = jnp.cos(ang)[:, None, :].astype(x.dtype)
    sin = jnp.sin(ang)[:, None, :].astype(x.dtype)
    x1, x2 = x[..., :half], x[..., half:]
    return jnp.concatenate([x1 * cos - x2 * sin, x1 * sin + x2 * cos], axis=-1)


def dwconv(x, w, b):
    out = lax.conv_general_dilated(x, w[:, None, :].astype(x.dtype), window_strides=(1,), padding="SAME",
                                   dimension_numbers=("NWC", "WIO", "NWC"),
                                   feature_group_count=x.shape[-1])
    return out + b.astype(x.dtype)


def decay_log(offset):
    return jnp.log(1.0 - 2.0 ** (-offset - jnp.arange(RET_HEADS, dtype=jnp.float32)))


def retention_scan(q, k, v, log_gamma, include_diag):
    B, S, H, dk = q.shape
    dv = v.shape[-1]
    C = RET_CHUNK
    N = S // C
    qc = q.reshape(B, N, C, H, dk).transpose(1, 0, 3, 2, 4)
    kc = k.reshape(B, N, C, H, dk).transpose(1, 0, 3, 2, 4)
    vc = v.reshape(B, N, C, H, dv).transpose(1, 0, 3, 2, 4)
    idx = jnp.arange(C, dtype=jnp.float32)
    diff = idx[:, None] - idx[None, :]
    mask = (diff >= 0) if include_diag else (diff > 0)
    decay_intra = jnp.where(mask[None], jnp.exp(log_gamma[:, None, None] * jnp.where(mask, diff, 0.0)[None]), 0.0)
    q_decay = jnp.exp(log_gamma[:, None] * (idx[None, :] + 1.0))[None, :, :, None]
    k_decay = jnp.exp(log_gamma[:, None] * (C - 1.0 - idx[None, :]))[None, :, :, None]
    chunk_decay = jnp.exp(log_gamma * C)[None, :, None, None]

    def step(state, inp):
        qb, kb, vb = inp
        scores = jnp.einsum("bhid,bhjd->bhij", qb, kb) * decay_intra[None]
        intra = jnp.einsum("bhij,bhjv->bhiv", scores, vb)
        cross = jnp.einsum("bhid,bhdv->bhiv", qb, state) * q_decay
        new_state = state * chunk_decay + jnp.einsum("bhjd,bhjv->bhdv", kb * k_decay, vb)
        return new_state, intra + cross

    state0 = jnp.zeros((B, H, dk, dv), jnp.float32)
    _, out = lax.scan(step, state0, (qc, kc, vc))
    return out.transpose(1, 0, 3, 2, 4).reshape(B, S, H, dv)


def retention_bidir(q, k, v):
    fwd = retention_scan(q, k, v, decay_log(5.0), True)
    bwd = retention_scan(q[:, ::-1], k[:, ::-1], v[:, ::-1], decay_log(5.5), False)[:, ::-1]
    return fwd + bwd


def mla_attend(qn, qr, kn, kr, v):
    B, S, H, dn = qn.shape
    dr = qr.shape[-1]
    dv = v.shape[-1]
    NB = S // Q_BLOCK
    scale = (MLA_NOPE + MLA_ROPE) ** -0.5
    qn_b = qn.reshape(B, NB, Q_BLOCK, H, dn).transpose(1, 0, 2, 3, 4)
    qr_b = qr.reshape(B, NB, Q_BLOCK, H, dr).transpose(1, 0, 2, 3, 4)

    def blk(args):
        qnb, qrb = args
        s = jnp.einsum("bqhd,bkhd->bhqk", qnb, kn) + jnp.einsum("bqhr,bkr->bhqk", qrb, kr)
        p = jax.nn.softmax(s.astype(jnp.float32) * scale, axis=-1)
        return jnp.einsum("bhqk,bkhv->bqhv", p.astype(v.dtype), v)

    o = lax.map(blk, (qn_b, qr_b))
    return o.transpose(1, 0, 2, 3, 4).reshape(B, S, H, dv)


def encoder_layer(x, attn_norm_w, w_in, ret_gn_w, mla_q_norm_w, mla_w_uq, mla_kv_norm_w, mla_w_ukv,
                  conv_dw_w, conv_dw_b, conv_ln_w, conv_ln_b, conv_pw_w, conv_pw_b, w_out,
                  ffn_norm_w, w_up, ffn_conv_w, ffn_conv_b, w_down):
    B, S, _ = x.shape
    pos = jnp.arange(S, dtype=jnp.float32)
    h = rmsnorm(x, attn_norm_w)
    proj = h @ w_in
    cuts = np.cumsum([RET_W, RET_W, RET_W, RET_W, MLA_Q_LORA, MLA_KV_LORA, MLA_ROPE]).tolist()
    q_r, k_r, v_r, g_r, cq, ckv, kr_raw, conv_in = jnp.split(proj, cuts, axis=-1)

    q = rope(q_r.reshape(B, S, RET_HEADS, RET_HEAD_DIM), pos).astype(jnp.float32)
    k = (rope(k_r.reshape(B, S, RET_HEADS, RET_HEAD_DIM), pos) * (RET_HEAD_DIM ** -0.5)).astype(jnp.float32)
    v = v_r.reshape(B, S, RET_HEADS, RET_HEAD_DIM).astype(jnp.float32)
    y = retention_bidir(q, k, v)
    mu = jnp.mean(y, axis=-1, keepdims=True)
    var = jnp.mean(jnp.square(y - mu), axis=-1, keepdims=True)
    y = ((y - mu) * lax.rsqrt(var + EPS)).reshape(B, S, RET_W) * ret_gn_w.astype(jnp.float32)
    y_ret = jax.nn.silu(g_r) * y.astype(x.dtype)

    cq_n = rmsnorm(cq, mla_q_norm_w)
    qm = (cq_n @ mla_w_uq).reshape(B, S, MLA_HEADS, MLA_NOPE + MLA_ROPE)
    qn = qm[..., :MLA_NOPE]
    qr = rope(qm[..., MLA_NOPE:], pos)
    ckv_n = rmsnorm(ckv, mla_kv_norm_w)
    kv = (ckv_n @ mla_w_ukv).reshape(B, S, MLA_HEADS, MLA_NOPE + MLA_V)
    kn = kv[..., :MLA_NOPE]
    vm = kv[..., MLA_NOPE:]
    kr = rope(kr_raw[:, :, None, :], pos)[:, :, 0, :]
    y_mla = mla_attend(qn, qr, kn, kr, vm).reshape(B, S, MLA_W)

    a, gt = jnp.split(conv_in, 2, axis=-1)
    c = a * jax.nn.sigmoid(gt)
    c = dwconv(c, conv_dw_w, conv_dw_b)
    c = jax.nn.silu(layernorm(c, conv_ln_w, conv_ln_b))
    y_conv = c @ conv_pw_w + conv_pw_b

    x = x + jnp.concatenate([y_ret, y_mla, y_conv], axis=-1) @ w_out

    h2 = rmsnorm(x, ffn_norm_w)
    u = dwconv(h2 @ w_up, ffn_conv_w, ffn_conv_b)
    gate, up = jnp.split(u, 2, axis=-1)
    return x + (jax.nn.silu(gate) * up) @ w_down


def setup_inputs(seed: int = 0) -> dict:
    key = jax.random.key(seed)
    ks = jax.random.split(key, 24)
    f32 = jnp.float32

    def nrm(k, shape, scale):
        return jax.random.normal(k, shape, f32) * scale

    def gain(k, shape):
        return 1.0 + 0.01 * jax.random.normal(k, shape, f32)

    L = DEPTH
    return {
        "x_prompt": jax.random.normal(ks[0], (BATCH, SEQ, D_MODEL), f32),
        "x_sample": jax.random.normal(ks[1], (DEC_BATCH, DEC_SEQ, D_MODEL), f32),
        "attn_norm_w": gain(ks[2], (L, D_MODEL)),
        "w_in": nrm(ks[3], (L, D_MODEL, IN_COLS), D_MODEL ** -0.5),
        "ret_gn_w": gain(ks[4], (L, RET_W)),
        "mla_q_norm_w": gain(ks[5], (L, MLA_Q_LORA)),
        "mla_w_uq": nrm(ks[6], (L, MLA_Q_LORA, MLA_HEADS * (MLA_NOPE + MLA_ROPE)), MLA_Q_LORA ** -0.5),
        "mla_kv_norm_w": gain(ks[7], (L, MLA_KV_LORA)),
        "mla_w_ukv": nrm(ks[8], (L, MLA_KV_LORA, MLA_HEADS * (MLA_NOPE + MLA_V)), MLA_KV_LORA ** -0.5),
        "conv_dw_w": nrm(ks[9], (L, CONV_K, CONV_CH), CONV_K ** -0.5),
        "conv_dw_b": nrm(ks[10], (L, CONV_CH), 0.01),
        "conv_ln_w": gain(ks[11], (L, CONV_CH)),
        "conv_ln_b": nrm(ks[12], (L, CONV_CH), 0.01),
        "conv_pw_w": nrm(ks[13], (L, CONV_CH, CONV_CH), CONV_CH ** -0.5),
        "conv_pw_b": nrm(ks[14], (L, CONV_CH), 0.01),
        "w_out": nrm(ks[15], (L, D_MODEL, D_MODEL), D_MODEL ** -0.5),
        "ffn_norm_w": gain(ks[16], (L, D_MODEL)),
        "w_up": nrm(ks[17], (L, D_MODEL, 2 * D_FF), D_MODEL ** -0.5),
        "ffn_conv_w": nrm(ks[18], (L, FFN_CONV_K, 2 * D_FF), FFN_CONV_K ** -0.5),
        "ffn_conv_b": nrm(ks[19], (L, 2 * D_FF), 0.01),
        "w_down": nrm(ks[20], (L, D_FF, D_MODEL), D_FF ** -0.5),
        "final_norm_w": gain(ks[21], (D_MODEL,)),
    }


def reference(x_prompt, x_sample, attn_norm_w, w_in, ret_gn_w, mla_q_norm_w, mla_w_uq, mla_kv_norm_w,
              mla_w_ukv, conv_dw_w, conv_dw_b, conv_ln_w, conv_ln_b, conv_pw_w, conv_pw_b, w_out,
              ffn_norm_w, w_up, ffn_conv_w, ffn_conv_b, w_down, final_norm_w):
    xp = x_prompt
    xs = x_sample
    for l in range(DEPTH):
        lw = (attn_norm_w[l], w_in[l], ret_gn_w[l], mla_q_norm_w[l], mla_w_uq[l], mla_kv_norm_w[l],
              mla_w_ukv[l], conv_dw_w[l], conv_dw_b[l], conv_ln_w[l], conv_ln_b[l], conv_pw_w[l],
              conv_pw_b[l], w_out[l], ffn_norm_w[l], w_up[l], ffn_conv_w[l], ffn_conv_b[l], w_down[l])
        xp = encoder_layer(xp, *lw)
        xs = encoder_layer(xs, *lw)
    y_prompt = rmsnorm(xp, final_norm_w)
    y_sample = rmsnorm(xs, final_norm_w)
    return (y_prompt, y_sample)
```

```python
import functools

import numpy as np
import jax
import jax.numpy as jnp
from jax import lax
from jax.experimental import pallas as pl
from jax.experimental.pallas import tpu as pltpu

F32 = jnp.float32
BF16 = jnp.bfloat16

LANE = 128
VMEM_LIMIT = 56 * 1024 * 1024

D_MODEL = 1024
RET_HEADS = 6
RET_HEAD_DIM = 64
RET_W = RET_HEADS * RET_HEAD_DIM
RET_CHUNK = 128
RET_PAIRS = RET_HEADS // 2
MLA_HEADS = 6
MLA_NOPE = 64
MLA_ROPE = 32
MLA_V = 64
MLA_Q_LORA = 384
MLA_KV_LORA = 128
MLA_W = MLA_HEADS * MLA_V
MLA_PAIRS = MLA_HEADS // 2
MLA_QK_W = MLA_HEADS * LANE
CONV_CH = D_MODEL - RET_W - MLA_W
CONV_K = 31
CONV_HALO = 16
D_FF = 2816
FF_CHUNK = 256
FFN_HALO = 16
ROPE_BASE = 10000.0
EPS = 1e-6

C_Q = 0
C_K = C_Q + RET_W
C_V = C_K + RET_W
C_G = C_V + RET_W
C_CQ = C_G + RET_W
C_CKV = C_CQ + MLA_Q_LORA
C_KR = C_CKV + MLA_KV_LORA
C_CONV = C_KR + LANE
IN_COLS_P = C_CONV + 2 * CONV_CH

NT_DIMS = (((1,), (1,)), ((), ()))
TN_DIMS = (((0,), (0,)), ((), ()))


def _params(*sem):
    return pltpu.CompilerParams(dimension_semantics=sem, vmem_limit_bytes=VMEM_LIMIT)


def _const_spec(shape):
    nd = len(shape)
    return pl.BlockSpec(shape, lambda *_: (0,) * nd)


def _rms(x, w):
    ms = jnp.mean(x * x, axis=-1, keepdims=True)
    return x * lax.rsqrt(ms + EPS) * w


def _sigmoid(x):
    return 1.0 / (1.0 + jnp.exp(-x))


def _rope_table_kernel(fr_ref, sr_ref, fm_ref, sm_ref, cr_ref, snr_ref, cm_ref, snm_ref, *, ts):
    pos = (pl.program_id(0) * ts + lax.broadcasted_iota(jnp.int32, (ts, LANE), 0)).astype(F32)
    ar = pos * fr_ref[...]
    cr_ref[...] = jnp.cos(ar)
    snr_ref[...] = jnp.sin(ar) * sr_ref[...]
    am = pos * fm_ref[...]
    cm_ref[...] = jnp.cos(am)
    snm_ref[...] = jnp.sin(am) * sm_ref[...]


def _rope_tables(S):
    c = np.arange(LANE)
    half_r = RET_HEAD_DIM // 2
    half_m = MLA_ROPE // 2
    freqs_r = 1.0 / (ROPE_BASE ** (jnp.arange(half_r, dtype=F32) / half_r))
    freqs_m = 1.0 / (ROPE_BASE ** (jnp.arange(half_m, dtype=F32) / half_m))
    fr = freqs_r[c % half_r][None, :]
    sr = jnp.asarray(np.where((c % RET_HEAD_DIM) < half_r, -1.0, 1.0), F32)[None, :]
    in_rope = (c >= MLA_NOPE) & (c < MLA_NOPE + MLA_ROPE)
    fm = jnp.where(in_rope, freqs_m[(c - MLA_NOPE) % half_m], 0.0)[None, :]
    sm = jnp.asarray(np.where(c < MLA_NOPE + half_m, -1.0, 1.0), F32)[None, :]
    ts = min(S, 1024)
    row = _const_spec((1, LANE))
    tab = pl.BlockSpec((ts, LANE), lambda i: (i, 0))
    return pl.pallas_call(
        functools.partial(_rope_table_kernel, ts=ts),
        grid=(S // ts,),
        in_specs=[row, row, row, row],
        out_specs=[tab, tab, tab, tab],
        out_shape=[jax.ShapeDtypeStruct((S, LANE), F32)] * 4,
        compiler_params=_params("parallel"),
    )(fr, sr, fm, sm)


def _inproj_kernel(x_ref, nw_ref, w_ref, cr_ref, sr_ref,
                   q_ref, k_ref, v_ref, g_ref, cq_ref, ckv_ref, krp_ref, cin_ref, *, tm):
    h = _rms(x_ref[0], nw_ref[...]).astype(BF16)

    def seg(lo, width):
        return jnp.dot(h, w_ref[:, lo:lo + width], preferred_element_type=F32)

    cr = cr_ref[...]
    sr = sr_ref[...]
    lane = lax.broadcasted_iota(jnp.int32, (tm, LANE), 1)
    first_half = (lane % RET_HEAD_DIM) < (RET_HEAD_DIM // 2)

    def rope_block(xb):
        partner = jnp.where(first_half, pltpu.roll(xb, LANE - 32, 1), pltpu.roll(xb, 32, 1))
        return xb * cr + partner * sr

    for b in range(RET_W // LANE):
        sl = slice(b * LANE, (b + 1) * LANE)
        q_ref[0, :, sl] = rope_block(seg(C_Q + b * LANE, LANE)).astype(BF16)
        k_ref[0, :, sl] = (rope_block(seg(C_K + b * LANE, LANE)) * (RET_HEAD_DIM ** -0.5)).astype(BF16)
    v_ref[0] = seg(C_V, RET_W).astype(BF16)
    g_ref[0] = seg(C_G, RET_W)
    cq_ref[0] = seg(C_CQ, MLA_Q_LORA)
    ckv_ref[0] = seg(C_CKV, MLA_KV_LORA)
    krp_ref[0] = seg(C_KR, LANE)
    cin_ref[0] = seg(C_CONV, 2 * CONV_CH)


def _inproj(x, norm_w, w_in_p, cos_r, sin_r, tm):
    B, S, _ = x.shape

    def tok(width):
        return pl.BlockSpec((1, tm, width), lambda b, i: (b, i, 0))

    tab = pl.BlockSpec((tm, LANE), lambda b, i: (i, 0))
    widths = [RET_W, RET_W, RET_W, RET_W, MLA_Q_LORA, MLA_KV_LORA, LANE, 2 * CONV_CH]
    dtypes = [BF16, BF16, BF16, F32, F32, F32, F32, F32]
    return pl.pallas_call(
        functools.partial(_inproj_kernel, tm=tm),
        grid=(B, S // tm),
        in_specs=[tok(D_MODEL), _const_spec((1, D_MODEL)), _const_spec((D_MODEL, IN_COLS_P)), tab, tab],
        out_specs=[tok(w) for w in widths],
        out_shape=[jax.ShapeDtypeStruct((B, S, w), d) for w, d in zip(widths, dtypes)],
        compiler_params=_params("parallel", "parallel"),
    )(x, norm_w, w_in_p, cos_r, sin_r)


def _retention_tables():
    C = RET_CHUNK
    hh = np.arange(RET_HEADS, dtype=np.float64)
    gf = 1.0 - 2.0 ** (-5.0 - hh)
    gb = 1.0 - 2.0 ** (-5.5 - hh)
    idx = np.arange(C, dtype=np.float64)
    diff = idx[:, None] - idx[None, :]
    dcomb = np.zeros((RET_HEADS, C, C))
    for h in range(RET_HEADS):
        dcomb[h] = np.where(diff >= 0, gf[h] ** np.maximum(diff, 0), gb[h] ** np.maximum(-diff, 0))
    head_of_lane = np.arange(LANE) // RET_HEAD_DIM

    def per_pair_rows(fn):
        out = np.zeros((RET_PAIRS, C, LANE))
        for p in range(RET_PAIRS):
            for e in range(2):
                out[p][:, head_of_lane == e] = fn(2 * p + e)[:, None]
        return out

    qdf = per_pair_rows(lambda h: gf[h] ** (idx + 1.0))
    qdb = per_pair_rows(lambda h: gb[h] ** (C - idx))
    kdf = per_pair_rows(lambda h: gf[h] ** (C - 1.0 - idx))
    kdb = per_pair_rows(lambda h: gb[h] ** idx)
    same_head = head_of_lane[:, None] == head_of_lane[None, :]
    cdf = np.zeros((RET_PAIRS, LANE, LANE))
    cdb = np.zeros((RET_PAIRS, LANE, LANE))
    for p in range(RET_PAIRS):
        for e in range(2):
            blk = (head_of_lane[:, None] == e) & (head_of_lane[None, :] == e)
            cdf[p][blk] = gf[2 * p + e] ** C
            cdb[p][blk] = gb[2 * p + e] ** C
    arrs = [dcomb, qdf, qdb, kdf, kdb, cdf, cdb, same_head.astype(np.float64)]
    return [jnp.asarray(a, F32) for a in arrs]


def _retention_kernel(q_ref, k_ref, v_ref, g_ref, gnw_ref, dcomb_ref, qdf_ref, qdb_ref, kdf_ref, kdb_ref,
                      cdf_ref, cdb_ref, bm_ref, o_ref, sf_ref, sb_ref, ub_ref, *, n_chunks):
    j = pl.program_id(1)
    C = RET_CHUNK
    lane = lax.broadcasted_iota(jnp.int32, (C, LANE), 1)
    low = lane < RET_HEAD_DIM

    @pl.when(j == 0)
    def _():
        sb_ref[...] = jnp.zeros_like(sb_ref)

    @pl.when(j == n_chunks)
    def _():
        sf_ref[...] = jnp.zeros_like(sf_ref)

    def pair(ref, p):
        return ref[0, :, p * LANE:(p + 1) * LANE]

    def kv_outer(kp, vp, kdecay):
        kd = (kp.astype(F32) * kdecay).astype(BF16)
        return lax.dot_general(kd, vp, TN_DIMS, preferred_element_type=F32) * bm_ref[...]

    @pl.when(j < n_chunks)
    def _backward_states():
        n = n_chunks - 1 - j
        for p in range(RET_PAIRS):
            state = sb_ref[p]
            ub_ref[n, p] = state.astype(BF16)
            sb_ref[p] = state * cdb_ref[p] + kv_outer(pair(k_ref, p), pair(v_ref, p), kdb_ref[p])

    @pl.when(j >= n_chunks)
    def _forward():
        n = j - n_chunks
        for p in range(RET_PAIRS):
            qp = pair(q_ref, p)
            kp = pair(k_ref, p)
            vp = pair(v_ref, p)
            halves = []
            for e in range(2):
                qe = jnp.where(low if e == 0 else jnp.logical_not(low), qp, jnp.zeros_like(qp))
                s = lax.dot_general(qe, kp, NT_DIMS, preferred_element_type=F32)
                pe = (s * dcomb_ref[2 * p + e]).astype(BF16)
                halves.append(jnp.dot(pe, vp, preferred_element_type=F32))
            y = jnp.where(low, halves[0], halves[1])
            state = sf_ref[p]
            y = y + jnp.dot(qp, state.astype(BF16), preferred_element_type=F32) * qdf_ref[p]
            y = y + jnp.dot(qp, ub_ref[n, p], preferred_element_type=F32) * qdb_ref[p]
            sf_ref[p] = state * cdf_ref[p] + kv_outer(kp, vp, kdf_ref[p])

            inv = 1.0 / RET_HEAD_DIM
            mu_lo = jnp.sum(jnp.where(low, y, 0.0), axis=-1, keepdims=True) * inv
            mu_hi = jnp.sum(jnp.where(low, 0.0, y), axis=-1, keepdims=True) * inv
            d = y - jnp.where(low, mu_lo, mu_hi)
            d2 = d * d
            var_lo = jnp.sum(jnp.where(low, d2, 0.0), axis=-1, keepdims=True) * inv
            var_hi = jnp.sum(jnp.where(low, 0.0, d2), axis=-1, keepdims=True) * inv
            yn = d * lax.rsqrt(jnp.where(low, var_lo, var_hi) + EPS) * gnw_ref[:, p * LANE:(p + 1) * LANE]
            g = pair(g_ref, p)
            o_ref[0, :, p * LANE:(p + 1) * LANE] = (g * _sigmoid(g) * yn).astype(BF16)


def _retention(q, k, v, g, gn_w, tables):
    B, S, _ = q.shape
    C = RET_CHUNK
    N = S // C

    fwd_only = pl.BlockSpec((1, C, RET_W), lambda b, j: (b, jnp.maximum(j - N, 0), 0))
    both = pl.BlockSpec((1, C, RET_W), lambda b, j: (b, jnp.where(j < N, N - 1 - j, j - N), 0))
    return pl.pallas_call(
        functools.partial(_retention_kernel, n_chunks=N),
        grid=(B, 2 * N),
        in_specs=[fwd_only, both, both, fwd_only, _const_spec((1, RET_W))] + [_const_spec(t.shape) for t in tables],
        out_specs=fwd_only,
        out_shape=jax.ShapeDtypeStruct((B, S, RET_W), BF16),
        scratch_shapes=[pltpu.VMEM((RET_PAIRS, LANE, LANE), F32),
                        pltpu.VMEM((RET_PAIRS, LANE, LANE), F32),
                        pltpu.VMEM((N, RET_PAIRS, LANE, LANE), BF16)],
        compiler_params=_params("parallel", "arbitrary"),
    )(q, k, v, g, gn_w, *tables)


def _mla_prep_kernel(cq_ref, ckv_ref, krp_ref, cm_ref, sm_ref, qnw_ref, kvnw_ref, wq_ref, wk_ref, wv_ref, vone_ref,
                     qo_ref, ko_ref, vo_ref, *, tm):
    cm = cm_ref[...]
    sm = sm_ref[...]
    lane = lax.broadcasted_iota(jnp.int32, (tm, LANE), 1)
    first_half = lane < MLA_NOPE + MLA_ROPE // 2

    def rope_block(xb):
        partner = jnp.where(first_half, pltpu.roll(xb, LANE - 16, 1), pltpu.roll(xb, 16, 1))
        return xb * cm + partner * sm

    scale = (MLA_NOPE + MLA_ROPE) ** -0.5
    cqn = _rms(cq_ref[0], qnw_ref[...]).astype(BF16)
    ckvn = _rms(ckv_ref[0], kvnw_ref[...]).astype(BF16)
    kr = rope_block(krp_ref[0])
    for h in range(MLA_HEADS):
        sl = slice(h * LANE, (h + 1) * LANE)
        qh = jnp.dot(cqn, wq_ref[:, sl], preferred_element_type=F32)
        qo_ref[0, :, sl] = (rope_block(qh) * scale).astype(BF16)
        ko_ref[0, :, sl] = (jnp.dot(ckvn, wk_ref[:, sl], preferred_element_type=F32) + kr).astype(BF16)
        vo_ref[0, :, sl] = (jnp.dot(ckvn, wv_ref[:, sl], preferred_element_type=F32) + vone_ref[:, sl]).astype(BF16)


def _mla_prep(cq, ckv, krp, cos_m, sin_m, qn_w, kvn_w, wq_p, wk_p, wv_p, vone, tm):
    B, S, _ = cq.shape

    def tok(width):
        return pl.BlockSpec((1, tm, width), lambda b, i: (b, i, 0))

    tab = pl.BlockSpec((tm, LANE), lambda b, i: (i, 0))
    return pl.pallas_call(
        functools.partial(_mla_prep_kernel, tm=tm),
        grid=(B, S // tm),
        in_specs=[tok(MLA_Q_LORA), tok(MLA_KV_LORA), tok(LANE), tab, tab,
                  _const_spec((1, MLA_Q_LORA)), _const_spec((1, MLA_KV_LORA)),
                  _const_spec((MLA_Q_LORA, MLA_QK_W)), _const_spec((MLA_KV_LORA, MLA_QK_W)),
                  _const_spec((MLA_KV_LORA, MLA_QK_W)), _const_spec((1, MLA_QK_W))],
        out_specs=[tok(MLA_QK_W)] * 3,
        out_shape=[jax.ShapeDtypeStruct((B, S, MLA_QK_W), BF16)] * 3,
        compiler_params=_params("parallel", "parallel"),
    )(cq, ckv, krp, cos_m, sin_m, qn_w, kvn_w, wq_p, wk_p, wv_p, vone)


def _attention_kernel(q_ref, k_ref, v_ref, o_ref, *, tq, tk, seq):
    outs = []
    for e in range(2):
        sl = slice(e * LANE, (e + 1) * LANE)
        q = q_ref[0, :, sl]

        def body(j, carry):
            m, acc = carry
            off = pl.multiple_of(j * tk, tk)
            k = k_ref[0, pl.ds(off, tk), sl]
            v = v_ref[0, pl.ds(off, tk), sl]
            s = lax.dot_general(q, k, NT_DIMS, preferred_element_type=F32)
            m_new = jnp.maximum(m, jnp.max(s, axis=1, keepdims=True))
            alpha = jnp.exp(m - m_new)
            p = jnp.exp(s - m_new).astype(BF16)
            return m_new, alpha * acc + jnp.dot(p, v, preferred_element_type=F32)

        m0 = jnp.full((tq, 1), -jnp.inf, F32)
        acc0 = jnp.zeros((tq, LANE), F32)
        _, acc = lax.fori_loop(0, seq // tk, body, (m0, acc0))
        den_lane = MLA_V if e == 0 else 0
        outs.append(acc / acc[:, den_lane:den_lane + 1])
    lane = lax.broadcasted_iota(jnp.int32, (tq, LANE), 1)
    o_ref[0] = jnp.where(lane < MLA_V, outs[0], outs[1]).astype(BF16)


def _attention(q, k, v, tq, tk):
    B, S, _ = q.shape
    pw = 2 * LANE
    return pl.pallas_call(
        functools.partial(_attention_kernel, tq=tq, tk=tk, seq=S),
        grid=(B, MLA_PAIRS, S // tq),
        in_specs=[pl.BlockSpec((1, tq, pw), lambda b, p, i: (b, i, p)),
                  pl.BlockSpec((1, S, pw), lambda b, p, i: (b, 0, p)),
                  pl.BlockSpec((1, S, pw), lambda b, p, i: (b, 0, p))],
        out_specs=pl.BlockSpec((1, tq, LANE), lambda b, p, i: (b, i, p)),
        out_shape=jax.ShapeDtypeStruct((B, S, MLA_W), BF16),
        compiler_params=_params("parallel", "parallel", "parallel"),
    )(q, k, v)


def _conv_kernel(prev_ref, cur_ref, next_ref, dww_ref, dwb_ref, lnw_ref, lnb_ref, pww_ref, pwb_ref,
                 o_ref, ext_ref, *, tm, n_tiles, rows):
    i = pl.program_id(1)

    def glu(z):
        return z[:, :CONV_CH] * _sigmoid(z[:, CONV_CH:])

    ext_ref[0:CONV_HALO] = jnp.where(i > 0, glu(prev_ref[0]), 0.0)
    ext_ref[CONV_HALO:CONV_HALO + tm] = glu(cur_ref[0])
    ext_ref[CONV_HALO + tm:2 * CONV_HALO + tm] = jnp.where(i < n_tiles - 1, glu(next_ref[0]), 0.0)

    base = CONV_HALO - CONV_K // 2
    for r in range(0, tm, rows):
        acc = jnp.zeros((rows, CONV_CH), F32) + dwb_ref[...]
        for t in range(CONV_K):
            acc = acc + dww_ref[t:t + 1, :] * ext_ref[pl.ds(r + base + t, rows), :]
        mu = jnp.mean(acc, axis=-1, keepdims=True)
        d = acc - mu
        var = jnp.mean(d * d, axis=-1, keepdims=True)
        c = d * lax.rsqrt(var + EPS) * lnw_ref[...] + lnb_ref[...]
        c = (c * _sigmoid(c)).astype(BF16)
        y = jnp.dot(c, pww_ref[...], preferred_element_type=F32) + pwb_ref[...]
        o_ref[0, r:r + rows, :] = y.astype(BF16)


def _conv_module(cin, dw_w, dw_b, ln_w, ln_b, pw_w, pw_b, tm):
    B, S, _ = cin.shape
    n_tiles = S // tm
    hb = tm // CONV_HALO
    last = S // CONV_HALO - 1
    width = 2 * CONV_CH
    row = _const_spec((1, CONV_CH))
    return pl.pallas_call(
        functools.partial(_conv_kernel, tm=tm, n_tiles=n_tiles, rows=64),
        grid=(B, n_tiles),
        in_specs=[pl.BlockSpec((1, CONV_HALO, width), lambda b, i: (b, jnp.maximum(i * hb - 1, 0), 0)),
                  pl.BlockSpec((1, tm, width), lambda b, i: (b, i, 0)),
                  pl.BlockSpec((1, CONV_HALO, width), lambda b, i: (b, jnp.minimum((i + 1) * hb, last), 0)),
                  _const_spec((CONV_K, CONV_CH)), row, row, row, _const_spec((CONV_CH, CONV_CH)), row],
        out_specs=pl.BlockSpec((1, tm, CONV_CH), lambda b, i: (b, i, 0)),
        out_shape=jax.ShapeDtypeStruct((B, S, CONV_CH), BF16),
        scratch_shapes=[pltpu.VMEM((tm + 2 * CONV_HALO, CONV_CH), F32)],
        compiler_params=_params("parallel", "parallel"),
    )(cin, cin, cin, dw_w, dw_b, ln_w, ln_b, pw_w, pw_b)


def _outproj_kernel(x_ref, yr_ref, ym_ref, yc_ref, wo_ref, o_ref):
    acc = x_ref[0]
    acc = acc + jnp.dot(yr_ref[0], wo_ref[0:RET_W, :], preferred_element_type=F32)
    acc = acc + jnp.dot(ym_ref[0], wo_ref[RET_W:RET_W + MLA_W, :], preferred_element_type=F32)
    acc = acc + jnp.dot(yc_ref[0], wo_ref[RET_W + MLA_W:, :], preferred_element_type=F32)
    o_ref[0] = acc


def _outproj(x, y_ret, y_mla, y_conv, wo, tm):
    B, S, _ = x.shape

    def tok(width):
        return pl.BlockSpec((1, tm, width), lambda b, i: (b, i, 0))

    return pl.pallas_call(
        _outproj_kernel,
        grid=(B, S // tm),
        in_specs=[tok(D_MODEL), tok(RET_W), tok(MLA_W), tok(CONV_CH), _const_spec((D_MODEL, D_MODEL))],
        out_specs=tok(D_MODEL),
        out_shape=jax.ShapeDtypeStruct((B, S, D_MODEL), F32),
        compiler_params=_params("parallel", "parallel"),
    )(x, y_ret, y_mla, y_conv, wo)


def _ffn_kernel(prev_ref, cur_ref, next_ref, nw_ref, wg_ref, wu_ref, cwg_ref, cwu_ref, cbg_ref, cbu_ref, wd_ref,
                fw_ref, o_ref, h_ref, ug_ref, uu_ref, *, tm, n_tiles, final_norm):
    i = pl.program_id(1)
    nw = nw_ref[...]
    h_ref[0:FFN_HALO] = jnp.where(i > 0, _rms(prev_ref[0], nw), 0.0).astype(BF16)
    h_ref[FFN_HALO:FFN_HALO + tm] = _rms(cur_ref[0], nw).astype(BF16)
    h_ref[FFN_HALO + tm:2 * FFN_HALO + tm] = jnp.where(i < n_tiles - 1, _rms(next_ref[0], nw), 0.0).astype(BF16)

    def conv3(u_ref, w_ref, b_ref, sl):
        w = w_ref[:, sl]
        return (w[0:1] * u_ref[pl.ds(FFN_HALO - 1, tm), :] + w[1:2] * u_ref[pl.ds(FFN_HALO, tm), :]
                + w[2:3] * u_ref[pl.ds(FFN_HALO + 1, tm), :] + b_ref[:, sl])

    o_ref[0] = cur_ref[0]
    for c in range(D_FF // FF_CHUNK):
        sl = slice(c * FF_CHUNK, (c + 1) * FF_CHUNK)
        h = h_ref[...]
        ug_ref[...] = jnp.dot(h, wg_ref[:, sl], preferred_element_type=F32)
        uu_ref[...] = jnp.dot(h, wu_ref[:, sl], preferred_element_type=F32)
        gate = conv3(ug_ref, cwg_ref, cbg_ref, sl)
        up = conv3(uu_ref, cwu_ref, cbu_ref, sl)
        act = (gate * _sigmoid(gate) * up).astype(BF16)
        o_ref[0] += jnp.dot(act, wd_ref[sl, :], preferred_element_type=F32)
    if final_norm:
        o_ref[0] = _rms(o_ref[0], fw_ref[...])


def _ffn(x, norm_w, wg, wu, cwg, cwu, cbg, cbu, wd, final_w, tm, final_norm):
    B, S, _ = x.shape
    n_tiles = S // tm
    hb = tm // FFN_HALO
    last = S // FFN_HALO - 1
    single = pl.Buffered(1)

    def resident(shape):
        return pl.BlockSpec(shape, lambda b, i: (0,) * len(shape), pipeline_mode=single)

    return pl.pallas_call(
        functools.partial(_ffn_kernel, tm=tm, n_tiles=n_tiles, final_norm=final_norm),
        grid=(B, n_tiles),
        in_specs=[pl.BlockSpec((1, FFN_HALO, D_MODEL), lambda b, i: (b, jnp.maximum(i * hb - 1, 0), 0)),
                  pl.BlockSpec((1, tm, D_MODEL), lambda b, i: (b, i, 0)),
                  pl.BlockSpec((1, FFN_HALO, D_MODEL), lambda b, i: (b, jnp.minimum((i + 1) * hb, last), 0)),
                  _const_spec((1, D_MODEL)),
                  resident((D_MODEL, D_FF)), resident((D_MODEL, D_FF)),
                  _const_spec((3, D_FF)), _const_spec((3, D_FF)), _const_spec((1, D_FF)), _const_spec((1, D_FF)),
                  resident((D_FF, D_MODEL)), _const_spec((1, D_MODEL))],
        out_specs=pl.BlockSpec((1, tm, D_MODEL), lambda b, i: (b, i, 0)),
        out_shape=jax.ShapeDtypeStruct((B, S, D_MODEL), F32),
        scratch_shapes=[pltpu.VMEM((tm + 2 * FFN_HALO, D_MODEL), BF16),
                        pltpu.VMEM((tm + 2 * FFN_HALO, FF_CHUNK), F32),
                        pltpu.VMEM((tm + 2 * FFN_HALO, FF_CHUNK), F32)],
        compiler_params=_params("parallel", "parallel"),
    )(x, x, x, norm_w, wg, wu, cwg, cwu, cbg, cbu, wd, final_w)


def _prep_layer_weights(w_in, mla_w_uq, mla_w_ukv):
    L = w_in.shape[0]
    c0 = 4 * RET_W
    c1 = c0 + MLA_Q_LORA
    c2 = c1 + MLA_KV_LORA
    c3 = c2 + MLA_ROPE
    z = lambda n: jnp.zeros((L, D_MODEL, n), w_in.dtype)
    w_in_p = jnp.concatenate([w_in[..., :c2], z(MLA_NOPE), w_in[..., c2:c3], z(LANE - MLA_NOPE - MLA_ROPE),
                              w_in[..., c3:]], axis=-1).astype(BF16)

    uq = mla_w_uq.reshape(L, MLA_Q_LORA, MLA_HEADS, MLA_NOPE + MLA_ROPE)
    wq_p = jnp.concatenate([uq, jnp.zeros((L, MLA_Q_LORA, MLA_HEADS, LANE - MLA_NOPE - MLA_ROPE), uq.dtype)],
                           axis=-1).reshape(L, MLA_Q_LORA, MLA_QK_W).astype(BF16)

    ukv = mla_w_ukv.reshape(L, MLA_KV_LORA, MLA_HEADS, MLA_NOPE + MLA_V)
    kn = ukv[..., :MLA_NOPE]
    vm = ukv[..., MLA_NOPE:]
    zk = jnp.zeros_like(kn)
    wk_p = jnp.concatenate([kn, zk], axis=-1).reshape(L, MLA_KV_LORA, MLA_QK_W).astype(BF16)
    zv = jnp.zeros_like(vm)
    even = jnp.concatenate([vm, zv], axis=-1)
    odd = jnp.concatenate([zv, vm], axis=-1)
    is_even = (jnp.arange(MLA_HEADS) % 2 == 0)[None, None, :, None]
    wv_p = jnp.where(is_even, even, odd).reshape(L, MLA_KV_LORA, MLA_QK_W).astype(BF16)

    vone = np.zeros((MLA_HEADS, LANE), np.float32)
    vone[0::2, MLA_V] = 1.0
    vone[1::2, 0] = 1.0
    return w_in_p, wq_p, wk_p, wv_p, jnp.asarray(vone.reshape(1, MLA_QK_W))


def _tile(S, want):
    return min(S, want)


def kernel(x_prompt, x_sample, attn_norm_w, w_in, ret_gn_w, mla_q_norm_w, mla_w_uq, mla_kv_norm_w, mla_w_ukv,
           conv_dw_w, conv_dw_b, conv_ln_w, conv_ln_b, conv_pw_w, conv_pw_b, w_out, ffn_norm_w, w_up, ffn_conv_w,
           ffn_conv_b, w_down, final_norm_w):
    depth = w_in.shape[0]
    w_in_p, wq_p, wk_p, wv_p, vone = _prep_layer_weights(w_in, mla_w_uq, mla_w_ukv)
    w_out_b = w_out.astype(BF16)
    w_gate_b = w_up[..., :D_FF].astype(BF16)
    w_upp_b = w_up[..., D_FF:].astype(BF16)
    w_down_b = w_down.astype(BF16)
    conv_pw_b16 = conv_pw_w.astype(BF16)
    ret_tables = _retention_tables()
    row = lambda a: a[None, :]

    def trunk(x):
        S = x.shape[1]
        cos_r, sin_r, cos_m, sin_m = _rope_tables(S)
        tm = _tile(S, 512)
        for l in range(depth):
            q, k, v, g, cq, ckv, krp, cin = _inproj(x, row(attn_norm_w[l]), w_in_p[l], cos_r, sin_r, tm)
            y_ret = _retention(q, k, v, g, row(ret_gn_w[l]), ret_tables)
            qh, kh, vh = _mla_prep(cq, ckv, krp, cos_m, sin_m, row(mla_q_norm_w[l]), row(mla_kv_norm_w[l]),
                                   wq_p[l], wk_p[l], wv_p[l], vone, tm)
            y_mla = _attention(qh, kh, vh, _tile(S, 512), _tile(S, 512))
            y_conv = _conv_module(cin, conv_dw_w[l], row(conv_dw_b[l]), row(conv_ln_w[l]), row(conv_ln_b[l]),
                                  conv_pw_b16[l], row(conv_pw_b[l]), tm)
            x = _outproj(x, y_ret, y_mla, y_conv, w_out_b[l], tm)
            x = _ffn(x, row(ffn_norm_w[l]), w_gate_b[l], w_upp_b[l], ffn_conv_w[l][:, :D_FF], ffn_conv_w[l][:, D_FF:],
                     row(ffn_conv_b[l][:D_FF]), row(ffn_conv_b[l][D_FF:]), w_down_b[l], row(final_norm_w), tm,
                     final_norm=(l == depth - 1))
        return x

    return trunk(x_prompt), trunk(x_sample)
```

```python
import functools

import numpy as np
import jax
import jax.numpy as jnp
from jax import lax
from jax.experimental import pallas as pl
from jax.experimental.pallas import tpu as pltpu

F32 = jnp.float32
BF16 = jnp.bfloat16

LANE = 128
VMEM_LIMIT = 56 * 1024 * 1024

D_MODEL = 1024
RET_HEADS = 6
RET_HEAD_DIM = 64
RET_W = RET_HEADS * RET_HEAD_DIM
RET_CHUNK = 128
RET_PAIRS = RET_HEADS // 2
MLA_HEADS = 6
MLA_NOPE = 64
MLA_ROPE = 32
MLA_V = 64
MLA_Q_LORA = 384
MLA_KV_LORA = 128
MLA_W = MLA_HEADS * MLA_V
MLA_PAIRS = MLA_HEADS // 2
MLA_QK_W = MLA_HEADS * LANE
CONV_CH = D_MODEL - RET_W - MLA_W
CONV_K = 31
CONV_HALO = 16
D_FF = 2816
FF_CHUNK = 256
FFN_HALO = 16
ROPE_BASE = 10000.0
EPS = 1e-6

C_Q = 0
C_K = C_Q + RET_W
C_V = C_K + RET_W
C_G = C_V + RET_W
C_CQ = C_G + RET_W
C_CKV = C_CQ + MLA_Q_LORA
C_KR = C_CKV + MLA_KV_LORA
C_CONV = C_KR + LANE
IN_COLS_P = C_CONV + 2 * CONV_CH

NT_DIMS = (((1,), (1,)), ((), ()))
TN_DIMS = (((0,), (0,)), ((), ()))


def _params(*sem):
    return pltpu.CompilerParams(dimension_semantics=sem, vmem_limit_bytes=VMEM_LIMIT)


def _const_spec(shape):
    nd = len(shape)
    return pl.BlockSpec(shape, lambda *_: (0,) * nd)


def _rms(x, w):
    ms = jnp.mean(x * x, axis=-1, keepdims=True)
    return x * lax.rsqrt(ms + EPS) * w


def _sigmoid(x):
    return 1.0 / (1.0 + jnp.exp(-x))


def _rope_table_kernel(fr_ref, sr_ref, fm_ref, sm_ref, fc_ref, cr_ref, snr_ref, cm_ref, snm_ref, ct_ref, snt_ref, *, ts):
    pos = (pl.program_id(0) * ts + lax.broadcasted_iota(jnp.int32, (ts, LANE), 0)).astype(F32)
    ar = pos * fr_ref[...]
    cr_ref[...] = jnp.cos(ar)
    snr_ref[...] = jnp.sin(ar) * sr_ref[...]
    am = pos * fm_ref[...]
    cm_ref[...] = jnp.cos(am)
    snm_ref[...] = jnp.sin(am) * sm_ref[...]
    half_m = MLA_ROPE // 2
    pos_t = (pl.program_id(0) * ts + lax.broadcasted_iota(jnp.int32, (half_m, ts), 1)).astype(F32)
    at = fc_ref[...] * pos_t
    ct_ref[...] = jnp.cos(at)
    snt_ref[...] = jnp.sin(at)


def _rope_tables(S):
    c = np.arange(LANE)
    half_r = RET_HEAD_DIM // 2
    half_m = MLA_ROPE // 2
    freqs_r = 1.0 / (ROPE_BASE ** (jnp.arange(half_r, dtype=F32) / half_r))
    freqs_m = 1.0 / (ROPE_BASE ** (jnp.arange(half_m, dtype=F32) / half_m))
    fr = freqs_r[c % half_r][None, :]
    sr = jnp.asarray(np.where((c % RET_HEAD_DIM) < half_r, -1.0, 1.0), F32)[None, :]
    in_rope = (c >= MLA_NOPE) & (c < MLA_NOPE + MLA_ROPE)
    fm = jnp.where(in_rope, freqs_m[(c - MLA_NOPE) % half_m], 0.0)[None, :]
    sm = jnp.asarray(np.where(c < MLA_NOPE + half_m, -1.0, 1.0), F32)[None, :]
    fc = freqs_m[:, None]
    ts = min(S, 1024)
    row = _const_spec((1, LANE))
    tab = pl.BlockSpec((ts, LANE), lambda i: (i, 0))
    tab_t = pl.BlockSpec((half_m, ts), lambda i: (0, i))
    return pl.pallas_call(
        functools.partial(_rope_table_kernel, ts=ts),
        grid=(S // ts,),
        in_specs=[row, row, row, row, _const_spec((half_m, 1))],
        out_specs=[tab, tab, tab, tab, tab_t, tab_t],
        out_shape=[jax.ShapeDtypeStruct((S, LANE), F32)] * 4 + [jax.ShapeDtypeStruct((half_m, S), F32)] * 2,
        compiler_params=_params("parallel"),
    )(fr, sr, fm, sm, fc)


def _inproj_kernel(x_ref, nw_ref, w_ref, cr_ref, sr_ref,
                   q_ref, k_ref, v_ref, g_ref, cq_ref, ckv_ref, krp_ref, cin_ref, *, tm):
    h = _rms(x_ref[0], nw_ref[...]).astype(BF16)

    def seg(lo, width):
        return jnp.dot(h, w_ref[:, lo:lo + width], preferred_element_type=F32)

    cr = cr_ref[...]
    sr = sr_ref[...]
    lane = lax.broadcasted_iota(jnp.int32, (tm, LANE), 1)
    first_half = (lane % RET_HEAD_DIM) < (RET_HEAD_DIM // 2)

    def rope_block(xb):
        partner = jnp.where(first_half, pltpu.roll(xb, LANE - 32, 1), pltpu.roll(xb, 32, 1))
        return xb * cr + partner * sr

    for b in range(RET_W // LANE):
        sl = slice(b * LANE, (b + 1) * LANE)
        q_ref[0, :, sl] = rope_block(seg(C_Q + b * LANE, LANE)).astype(BF16)
        k_ref[0, :, sl] = (rope_block(seg(C_K + b * LANE, LANE)) * (RET_HEAD_DIM ** -0.5)).astype(BF16)
    v_ref[0] = seg(C_V, RET_W).astype(BF16)
    g_ref[0] = seg(C_G, RET_W)
    cq_ref[0] = seg(C_CQ, MLA_Q_LORA)
    ckv_ref[0] = seg(C_CKV, MLA_KV_LORA)
    krp_ref[0] = seg(C_KR, LANE)
    cin_ref[0] = seg(C_CONV, 2 * CONV_CH)


def _inproj(x, norm_w, w_in_p, cos_r, sin_r, tm):
    B, S, _ = x.shape

    def tok(width):
        return pl.BlockSpec((1, tm, width), lambda b, i: (b, i, 0))

    tab = pl.BlockSpec((tm, LANE), lambda b, i: (i, 0))
    widths = [RET_W, RET_W, RET_W, RET_W, MLA_Q_LORA, MLA_KV_LORA, LANE, 2 * CONV_CH]
    dtypes = [BF16, BF16, BF16, F32, F32, F32, F32, F32]
    return pl.pallas_call(
        functools.partial(_inproj_kernel, tm=tm),
        grid=(B, S // tm),
        in_specs=[tok(D_MODEL), _const_spec((1, D_MODEL)), _const_spec((D_MODEL, IN_COLS_P)), tab, tab],
        out_specs=[tok(w) for w in widths],
        out_shape=[jax.ShapeDtypeStruct((B, S, w), d) for w, d in zip(widths, dtypes)],
        compiler_params=_params("parallel", "parallel"),
    )(x, norm_w, w_in_p, cos_r, sin_r)


def _retention_tables():
    C = RET_CHUNK
    hh = np.arange(RET_HEADS, dtype=np.float64)
    gf = 1.0 - 2.0 ** (-5.0 - hh)
    gb = 1.0 - 2.0 ** (-5.5 - hh)
    idx = np.arange(C, dtype=np.float64)
    diff = idx[:, None] - idx[None, :]
    dcomb = np.zeros((RET_HEADS, C, C))
    for h in range(RET_HEADS):
        dcomb[h] = np.where(diff >= 0, gf[h] ** np.maximum(diff, 0), gb[h] ** np.maximum(-diff, 0))
    head_of_lane = np.arange(LANE) // RET_HEAD_DIM

    def per_pair_rows(fn):
        out = np.zeros((RET_PAIRS, C, LANE))
        for p in range(RET_PAIRS):
            for e in range(2):
                out[p][:, head_of_lane == e] = fn(2 * p + e)[:, None]
        return out

    qdf = per_pair_rows(lambda h: gf[h] ** (idx + 1.0))
    qdb = per_pair_rows(lambda h: gb[h] ** (C - idx))
    kdf = per_pair_rows(lambda h: gf[h] ** (C - 1.0 - idx))
    kdb = per_pair_rows(lambda h: gb[h] ** idx)
    same_head = head_of_lane[:, None] == head_of_lane[None, :]
    cdf = np.zeros((RET_PAIRS, LANE, LANE))
    cdb = np.zeros((RET_PAIRS, LANE, LANE))
    for p in range(RET_PAIRS):
        for e in range(2):
            blk = (head_of_lane[:, None] == e) & (head_of_lane[None, :] == e)
            cdf[p][blk] = gf[2 * p + e] ** C
            cdb[p][blk] = gb[2 * p + e] ** C
    arrs = [dcomb, qdf, qdb, kdf, kdb, cdf, cdb, same_head.astype(np.float64)]
    return [jnp.asarray(a, F32) for a in arrs]


def _retention_kernel(q_ref, k_ref, v_ref, g_ref, gnw_ref, dcomb_ref, qdf_ref, qdb_ref, kdf_ref, kdb_ref,
                      cdf_ref, cdb_ref, bm_ref, o_ref, sf_ref, sb_ref, ub_ref, *, n_chunks):
    j = pl.program_id(1)
    C = RET_CHUNK
    lane = lax.broadcasted_iota(jnp.int32, (C, LANE), 1)
    low = lane < RET_HEAD_DIM

    @pl.when(j == 0)
    def _():
        sb_ref[...] = jnp.zeros_like(sb_ref)

    @pl.when(j == n_chunks)
    def _():
        sf_ref[...] = jnp.zeros_like(sf_ref)

    def pair(ref, p):
        return ref[0, :, p * LANE:(p + 1) * LANE]

    def kv_outer(kp, vp, kdecay):
        kd = (kp.astype(F32) * kdecay).astype(BF16)
        return lax.dot_general(kd, vp, TN_DIMS, preferred_element_type=F32) * bm_ref[...]

    @pl.when(j < n_chunks)
    def _backward_states():
        n = n_chunks - 1 - j
        for p in range(RET_PAIRS):
            state = sb_ref[p]
            ub_ref[n, p] = state.astype(BF16)
            sb_ref[p] = state * cdb_ref[p] + kv_outer(pair(k_ref, p), pair(v_ref, p), kdb_ref[p])

    @pl.when(j >= n_chunks)
    def _forward():
        n = j - n_chunks
        for p in range(RET_PAIRS):
            qp = pair(q_ref, p)
            kp = pair(k_ref, p)
            vp = pair(v_ref, p)
            halves = []
            for e in range(2):
                qe = jnp.where(low if e == 0 else jnp.logical_not(low), qp, jnp.zeros_like(qp))
                s = lax.dot_general(qe, kp, NT_DIMS, preferred_element_type=F32)
                pe = (s * dcomb_ref[2 * p + e]).astype(BF16)
                halves.append(jnp.dot(pe, vp, preferred_element_type=F32))
            y = jnp.where(low, halves[0], halves[1])
            state = sf_ref[p]
            y = y + jnp.dot(qp, state.astype(BF16), preferred_element_type=F32) * qdf_ref[p]
            y = y + jnp.dot(qp, ub_ref[n, p], preferred_element_type=F32) * qdb_ref[p]
            sf_ref[p] = state * cdf_ref[p] + kv_outer(kp, vp, kdf_ref[p])

            inv = 1.0 / RET_HEAD_DIM
            mu_lo = jnp.sum(jnp.where(low, y, 0.0), axis=-1, keepdims=True) * inv
            mu_hi = jnp.sum(jnp.where(low, 0.0, y), axis=-1, keepdims=True) * inv
            d = y - jnp.where(low, mu_lo, mu_hi)
            d2 = d * d
            var_lo = jnp.sum(jnp.where(low, d2, 0.0), axis=-1, keepdims=True) * inv
            var_hi = jnp.sum(jnp.where(low, 0.0, d2), axis=-1, keepdims=True) * inv
            yn = d * lax.rsqrt(jnp.where(low, var_lo, var_hi) + EPS) * gnw_ref[:, p * LANE:(p + 1) * LANE]
            g = pair(g_ref, p)
            o_ref[0, :, p * LANE:(p + 1) * LANE] = (g * _sigmoid(g) * yn).astype(BF16)


def _retention(q, k, v, g, gn_w, tables):
    B, S, _ = q.shape
    C = RET_CHUNK
    N = S // C

    fwd_only = pl.BlockSpec((1, C, RET_W), lambda b, j: (b, jnp.maximum(j - N, 0), 0))
    both = pl.BlockSpec((1, C, RET_W), lambda b, j: (b, jnp.where(j < N, N - 1 - j, j - N), 0))
    return pl.pallas_call(
        functools.partial(_retention_kernel, n_chunks=N),
        grid=(B, 2 * N),
        in_specs=[fwd_only, both, both, fwd_only, _const_spec((1, RET_W))] + [_const_spec(t.shape) for t in tables],
        out_specs=fwd_only,
        out_shape=jax.ShapeDtypeStruct((B, S, RET_W), BF16),
        scratch_shapes=[pltpu.VMEM((RET_PAIRS, LANE, LANE), F32),
                        pltpu.VMEM((RET_PAIRS, LANE, LANE), F32),
                        pltpu.VMEM((N, RET_PAIRS, LANE, LANE), BF16)],
        compiler_params=_params("parallel", "arbitrary"),
    )(q, k, v, g, gn_w, *tables)


def _mla_prep_kernel(cq_ref, ckv_ref, krp_ref, cm_ref, sm_ref, ct_ref, st_ref, qnw_ref, kvnw_ref,
                     wqt_ref, wk_ref, wvt_ref, vone_ref, qt_ref, ko_ref, vt_ref, *, tm):
    cm = cm_ref[...]
    sm = sm_ref[...]
    lane = lax.broadcasted_iota(jnp.int32, (tm, LANE), 1)
    first_half = lane < MLA_NOPE + MLA_ROPE // 2
    partner_of = lambda xb: jnp.where(first_half, pltpu.roll(xb, LANE - 16, 1), pltpu.roll(xb, 16, 1))

    cqn = _rms(cq_ref[0], qnw_ref[...]).astype(BF16)
    ckvn = _rms(ckv_ref[0], kvnw_ref[...]).astype(BF16)
    krp = krp_ref[0]
    kr = krp * cm + partner_of(krp) * sm

    ct = ct_ref[...]
    st = st_ref[...]
    qscale = (MLA_NOPE + MLA_ROPE) ** -0.5 * np.log2(np.e)
    half = MLA_ROPE // 2
    r0 = MLA_NOPE
    for h in range(MLA_HEADS):
        base = h * LANE
        rows = slice(base, base + LANE)
        qt = lax.dot_general(wqt_ref[rows, :], cqn, NT_DIMS, preferred_element_type=F32)
        x1 = qt[r0:r0 + half]
        x2 = qt[r0 + half:r0 + 2 * half]
        qt_ref[0, base:base + r0, :] = (qt[0:r0] * qscale).astype(BF16)
        qt_ref[0, base + r0:base + r0 + half, :] = ((x1 * ct - x2 * st) * qscale).astype(BF16)
        qt_ref[0, base + r0 + half:base + r0 + 2 * half, :] = ((x2 * ct + x1 * st) * qscale).astype(BF16)
        qt_ref[0, base + r0 + 2 * half:base + LANE, :] = jnp.zeros((LANE - r0 - 2 * half, tm), BF16)
        ko_ref[0, :, rows] = (jnp.dot(ckvn, wk_ref[:, rows], preferred_element_type=F32) + kr).astype(BF16)
        vt = lax.dot_general(wvt_ref[rows, :], ckvn, NT_DIMS, preferred_element_type=F32)
        vt_ref[0, rows, :] = (vt + vone_ref[rows, :]).astype(BF16)


def _mla_prep(cq, ckv, krp, cos_m, sin_m, cos_t, sin_t, qn_w, kvn_w, wqt_p, wk_p, wvt_p, vone, tm):
    B, S, _ = cq.shape

    def tok(width):
        return pl.BlockSpec((1, tm, width), lambda b, i: (b, i, 0))

    tab = pl.BlockSpec((tm, LANE), lambda b, i: (i, 0))
    tab_t = pl.BlockSpec((MLA_ROPE // 2, tm), lambda b, i: (0, i))
    feat = pl.BlockSpec((1, MLA_QK_W, tm), lambda b, i: (b, 0, i))
    return pl.pallas_call(
        functools.partial(_mla_prep_kernel, tm=tm),
        grid=(B, S // tm),
        in_specs=[tok(MLA_Q_LORA), tok(MLA_KV_LORA), tok(LANE), tab, tab, tab_t, tab_t,
                  _const_spec((1, MLA_Q_LORA)), _const_spec((1, MLA_KV_LORA)),
                  _const_spec((MLA_QK_W, MLA_Q_LORA)), _const_spec((MLA_KV_LORA, MLA_QK_W)),
                  _const_spec((MLA_QK_W, MLA_KV_LORA)), _const_spec((MLA_QK_W, 1))],
        out_specs=[feat, tok(MLA_QK_W), feat],
        out_shape=[jax.ShapeDtypeStruct((B, MLA_QK_W, S), BF16), jax.ShapeDtypeStruct((B, S, MLA_QK_W), BF16),
                   jax.ShapeDtypeStruct((B, MLA_QK_W, S), BF16)],
        compiler_params=_params("parallel", "parallel"),
    )(cq, ckv, krp, cos_m, sin_m, cos_t, sin_t, qn_w, kvn_w, wqt_p, wk_p, wvt_p, vone)


def _attention_kernel(qt_ref, k_ref, vt_ref, o_ref, acc_ref, s_ref, *, tq, tk, seq):
    acc_ref[...] = jnp.zeros_like(acc_ref)
    qts = [qt_ref[0, e * LANE:(e + 1) * LANE, :] for e in range(2)]
    heads = [slice(e * LANE, (e + 1) * LANE) for e in range(2)]
    n_trips = seq // (2 * tk)

    def scores(slot, j):
        off = pl.multiple_of(j * tk, tk)
        for e in range(2):
            s_ref[slot, e] = jnp.dot(k_ref[0, pl.ds(off, tk), heads[e]], qts[e], preferred_element_type=F32)

    def softmax_pv(slot, j, ms):
        off = pl.multiple_of(j * tk, tk)
        new_ms = []
        for e in range(2):
            st = s_ref[slot, e]
            m_new = jnp.maximum(ms[e], jnp.max(st, axis=0, keepdims=True))
            alpha = jnp.exp2(ms[e] - m_new)
            pt = jnp.exp2(st - m_new).astype(BF16)
            pv = jnp.dot(vt_ref[0, heads[e], pl.ds(off, tk)], pt, preferred_element_type=F32)
            acc_ref[e] = alpha * acc_ref[e] + pv
            new_ms.append(m_new)
        return tuple(new_ms)

    def trip(t, ms, last):
        j0 = 2 * t
        scores(1, j0 + 1)
        ms = softmax_pv(0, j0, ms)
        if not last:
            scores(0, j0 + 2)
        return softmax_pv(1, j0 + 1, ms)

    scores(0, 0)
    m0 = jnp.full((1, tq), -jnp.inf, F32)
    ms = lax.fori_loop(0, n_trips - 1, lambda t, ms: trip(t, ms, False), (m0, m0))
    trip(n_trips - 1, ms, True)
    outs = []
    for e in range(2):
        acc = acc_ref[e]
        outs.append(acc[0:MLA_V] / acc[MLA_V:MLA_V + 1])
    o_ref[0] = jnp.concatenate(outs, axis=0).T.astype(BF16)


def _attention(qt, k, vt, tq, tk):
    B, S, _ = k.shape
    pw = 2 * LANE
    return pl.pallas_call(
        functools.partial(_attention_kernel, tq=tq, tk=tk, seq=S),
        grid=(B, MLA_PAIRS, S // tq),
        in_specs=[pl.BlockSpec((1, pw, tq), lambda b, p, i: (b, p, i)),
                  pl.BlockSpec((1, S, pw), lambda b, p, i: (b, 0, p)),
                  pl.BlockSpec((1, pw, S), lambda b, p, i: (b, p, 0))],
        out_specs=pl.BlockSpec((1, tq, LANE), lambda b, p, i: (b, i, p)),
        out_shape=jax.ShapeDtypeStruct((B, S, MLA_W), BF16),
        scratch_shapes=[pltpu.VMEM((2, LANE, tq), F32), pltpu.VMEM((2, 2, tk, tq), F32)],
        compiler_params=_params("parallel", "parallel", "parallel"),
    )(qt, k, vt)


def _conv_kernel(prev_ref, cur_ref, next_ref, dww_ref, dwb_ref, lnw_ref, lnb_ref, pww_ref, pwb_ref,
                 o_ref, ext_ref, *, tm, n_tiles, rows):
    i = pl.program_id(1)

    def glu(z):
        return z[:, :CONV_CH] * _sigmoid(z[:, CONV_CH:])

    ext_ref[0:CONV_HALO] = jnp.where(i > 0, glu(prev_ref[0]), 0.0)
    ext_ref[CONV_HALO:CONV_HALO + tm] = glu(cur_ref[0])
    ext_ref[CONV_HALO + tm:2 * CONV_HALO + tm] = jnp.where(i < n_tiles - 1, glu(next_ref[0]), 0.0)

    base = CONV_HALO - CONV_K // 2
    for r in range(0, tm, rows):
        acc = jnp.zeros((rows, CONV_CH), F32) + dwb_ref[...]
        for t in range(CONV_K):
            acc = acc + dww_ref[t:t + 1, :] * ext_ref[pl.ds(r + base + t, rows), :]
        mu = jnp.mean(acc, axis=-1, keepdims=True)
        d = acc - mu
        var = jnp.mean(d * d, axis=-1, keepdims=True)
        c = d * lax.rsqrt(var + EPS) * lnw_ref[...] + lnb_ref[...]
        c = (c * _sigmoid(c)).astype(BF16)
        y = jnp.dot(c, pww_ref[...], preferred_element_type=F32) + pwb_ref[...]
        o_ref[0, r:r + rows, :] = y.astype(BF16)


def _conv_module(cin, dw_w, dw_b, ln_w, ln_b, pw_w, pw_b, tm):
    B, S, _ = cin.shape
    n_tiles = S // tm
    hb = tm // CONV_HALO
    last = S // CONV_HALO - 1
    width = 2 * CONV_CH
    row = _const_spec((1, CONV_CH))
    return pl.pallas_call(
        functools.partial(_conv_kernel, tm=tm, n_tiles=n_tiles, rows=64),
        grid=(B, n_tiles),
        in_specs=[pl.BlockSpec((1, CONV_HALO, width), lambda b, i: (b, jnp.maximum(i * hb - 1, 0), 0)),
                  pl.BlockSpec((1, tm, width), lambda b, i: (b, i, 0)),
                  pl.BlockSpec((1, CONV_HALO, width), lambda b, i: (b, jnp.minimum((i + 1) * hb, last), 0)),
                  _const_spec((CONV_K, CONV_CH)), row, row, row, _const_spec((CONV_CH, CONV_CH)), row],
        out_specs=pl.BlockSpec((1, tm, CONV_CH), lambda b, i: (b, i, 0)),
        out_shape=jax.ShapeDtypeStruct((B, S, CONV_CH), BF16),
        scratch_shapes=[pltpu.VMEM((tm + 2 * CONV_HALO, CONV_CH), F32)],
        compiler_params=_params("parallel", "parallel"),
    )(cin, cin, cin, dw_w, dw_b, ln_w, ln_b, pw_w, pw_b)


def _outproj_kernel(x_ref, yr_ref, ym_ref, yc_ref, wo_ref, o_ref):
    acc = x_ref[0]
    acc = acc + jnp.dot(yr_ref[0], wo_ref[0:RET_W, :], preferred_element_type=F32)
    acc = acc + jnp.dot(ym_ref[0], wo_ref[RET_W:RET_W + MLA_W, :], preferred_element_type=F32)
    acc = acc + jnp.dot(yc_ref[0], wo_ref[RET_W + MLA_W:, :], preferred_element_type=F32)
    o_ref[0] = acc


def _outproj(x, y_ret, y_mla, y_conv, wo, tm):
    B, S, _ = x.shape

    def tok(width):
        return pl.BlockSpec((1, tm, width), lambda b, i: (b, i, 0))

    return pl.pallas_call(
        _outproj_kernel,
        grid=(B, S // tm),
        in_specs=[tok(D_MODEL), tok(RET_W), tok(MLA_W), tok(CONV_CH), _const_spec((D_MODEL, D_MODEL))],
        out_specs=tok(D_MODEL),
        out_shape=jax.ShapeDtypeStruct((B, S, D_MODEL), F32),
        compiler_params=_params("parallel", "parallel"),
    )(x, y_ret, y_mla, y_conv, wo)


def _ffn_kernel(prev_ref, cur_ref, next_ref, nw_ref, wg_ref, wu_ref, cwg_ref, cwu_ref, cbg_ref, cbu_ref, wd_ref,
                fw_ref, o_ref, h_ref, ug_ref, uu_ref, *, tm, n_tiles, final_norm):
    i = pl.program_id(1)
    nw = nw_ref[...]
    h_ref[0:FFN_HALO] = jnp.where(i > 0, _rms(prev_ref[0], nw), 0.0).astype(BF16)
    h_ref[FFN_HALO:FFN_HALO + tm] = _rms(cur_ref[0], nw).astype(BF16)
    h_ref[FFN_HALO + tm:2 * FFN_HALO + tm] = jnp.where(i < n_tiles - 1, _rms(next_ref[0], nw), 0.0).astype(BF16)

    def conv3(u_ref, w_ref, b_ref, sl):
        w = w_ref[:, sl]
        return (w[0:1] * u_ref[pl.ds(FFN_HALO - 1, tm), :] + w[1:2] * u_ref[pl.ds(FFN_HALO, tm), :]
                + w[2:3] * u_ref[pl.ds(FFN_HALO + 1, tm), :] + b_ref[:, sl])

    o_ref[0] = cur_ref[0]
    for c in range(D_FF // FF_CHUNK):
        sl = slice(c * FF_CHUNK, (c + 1) * FF_CHUNK)
        h = h_ref[...]
        ug_ref[...] = jnp.dot(h, wg_ref[:, sl], preferred_element_type=F32)
        uu_ref[...] = jnp.dot(h, wu_ref[:, sl], preferred_element_type=F32)
        gate = conv3(ug_ref, cwg_ref, cbg_ref, sl)
        up = conv3(uu_ref, cwu_ref, cbu_ref, sl)
        act = (gate * _sigmoid(gate) * up).astype(BF16)
        o_ref[0] += jnp.dot(act, wd_ref[sl, :], preferred_element_type=F32)
    if final_norm:
        o_ref[0] = _rms(o_ref[0], fw_ref[...])


def _ffn(x, norm_w, wg, wu, cwg, cwu, cbg, cbu, wd, final_w, tm, final_norm):
    B, S, _ = x.shape
    n_tiles = S // tm
    hb = tm // FFN_HALO
    last = S // FFN_HALO - 1
    single = pl.Buffered(1)

    def resident(shape):
        return pl.BlockSpec(shape, lambda b, i: (0,) * len(shape), pipeline_mode=single)

    return pl.pallas_call(
        functools.partial(_ffn_kernel, tm=tm, n_tiles=n_tiles, final_norm=final_norm),
        grid=(B, n_tiles),
        in_specs=[pl.BlockSpec((1, FFN_HALO, D_MODEL), lambda b, i: (b, jnp.maximum(i * hb - 1, 0), 0)),
                  pl.BlockSpec((1, tm, D_MODEL), lambda b, i: (b, i, 0)),
                  pl.BlockSpec((1, FFN_HALO, D_MODEL), lambda b, i: (b, jnp.minimum((i + 1) * hb, last), 0)),
                  _const_spec((1, D_MODEL)),
                  resident((D_MODEL, D_FF)), resident((D_MODEL, D_FF)),
                  _const_spec((3, D_FF)), _const_spec((3, D_FF)), _const_spec((1, D_FF)), _const_spec((1, D_FF)),
                  resident((D_FF, D_MODEL)), _const_spec((1, D_MODEL))],
        out_specs=pl.BlockSpec((1, tm, D_MODEL), lambda b, i: (b, i, 0)),
        out_shape=jax.ShapeDtypeStruct((B, S, D_MODEL), F32),
        scratch_shapes=[pltpu.VMEM((tm + 2 * FFN_HALO, D_MODEL), BF16),
                        pltpu.VMEM((tm + 2 * FFN_HALO, FF_CHUNK), F32),
                        pltpu.VMEM((tm + 2 * FFN_HALO, FF_CHUNK), F32)],
        compiler_params=_params("parallel", "parallel"),
    )(x, x, x, norm_w, wg, wu, cwg, cwu, cbg, cbu, wd, final_w)


def _prep_layer_weights(w_in, mla_w_uq, mla_w_ukv):
    L = w_in.shape[0]
    c0 = 4 * RET_W
    c1 = c0 + MLA_Q_LORA
    c2 = c1 + MLA_KV_LORA
    c3 = c2 + MLA_ROPE
    z = lambda n: jnp.zeros((L, D_MODEL, n), w_in.dtype)
    w_in_p = jnp.concatenate([w_in[..., :c2], z(MLA_NOPE), w_in[..., c2:c3], z(LANE - MLA_NOPE - MLA_ROPE),
                              w_in[..., c3:]], axis=-1).astype(BF16)

    uq = mla_w_uq.reshape(L, MLA_Q_LORA, MLA_HEADS, MLA_NOPE + MLA_ROPE)
    wq_p = jnp.concatenate([uq, jnp.zeros((L, MLA_Q_LORA, MLA_HEADS, LANE - MLA_NOPE - MLA_ROPE), uq.dtype)],
                           axis=-1).reshape(L, MLA_Q_LORA, MLA_QK_W)
    wqt_p = jnp.swapaxes(wq_p, 1, 2).astype(BF16)

    ukv = mla_w_ukv.reshape(L, MLA_KV_LORA, MLA_HEADS, MLA_NOPE + MLA_V)
    kn = ukv[..., :MLA_NOPE]
    vm = ukv[..., MLA_NOPE:]
    wk_p = jnp.concatenate([kn, jnp.zeros_like(kn)], axis=-1).reshape(L, MLA_KV_LORA, MLA_QK_W).astype(BF16)
    wv_p = jnp.concatenate([vm, jnp.zeros_like(vm)], axis=-1).reshape(L, MLA_KV_LORA, MLA_QK_W)
    wvt_p = jnp.swapaxes(wv_p, 1, 2).astype(BF16)

    vone = np.zeros((MLA_HEADS, LANE, 1), np.float32)
    vone[:, MLA_V, 0] = 1.0
    return w_in_p, wqt_p, wk_p, wvt_p, jnp.asarray(vone.reshape(MLA_QK_W, 1))


def _tile(S, want):
    return min(S, want)


def kernel(x_prompt, x_sample, attn_norm_w, w_in, ret_gn_w, mla_q_norm_w, mla_w_uq, mla_kv_norm_w, mla_w_ukv,
           conv_dw_w, conv_dw_b, conv_ln_w, conv_ln_b, conv_pw_w, conv_pw_b, w_out, ffn_norm_w, w_up, ffn_conv_w,
           ffn_conv_b, w_down, final_norm_w):
    depth = w_in.shape[0]
    w_in_p, wqt_p, wk_p, wvt_p, vone = _prep_layer_weights(w_in, mla_w_uq, mla_w_ukv)
    w_out_b = w_out.astype(BF16)
    w_gate_b = w_up[..., :D_FF].astype(BF16)
    w_upp_b = w_up[..., D_FF:].astype(BF16)
    w_down_b = w_down.astype(BF16)
    conv_pw_b16 = conv_pw_w.astype(BF16)
    ret_tables = _retention_tables()
    row = lambda a: a[None, :]

    def trunk(x):
        S = x.shape[1]
        cos_r, sin_r, cos_m, sin_m, cos_t, sin_t = _rope_tables(S)
        tm = _tile(S, 512)
        for l in range(depth):
            q, k, v, g, cq, ckv, krp, cin = _inproj(x, row(attn_norm_w[l]), w_in_p[l], cos_r, sin_r, tm)
            y_ret = _retention(q, k, v, g, row(ret_gn_w[l]), ret_tables)
            qh, kh, vh = _mla_prep(cq, ckv, krp, cos_m, sin_m, cos_t, sin_t, row(mla_q_norm_w[l]),
                                   row(mla_kv_norm_w[l]), wqt_p[l], wk_p[l], wvt_p[l], vone, tm)
            y_mla = _attention(qh, kh, vh, _tile(S, 512), _tile(S, 512))
            y_conv = _conv_module(cin, conv_dw_w[l], row(conv_dw_b[l]), row(conv_ln_w[l]), row(conv_ln_b[l]),
                                  conv_pw_b16[l], row(conv_pw_b[l]), tm)
            x = _outproj(x, y_ret, y_mla, y_conv, w_out_b[l], tm)
            x = _ffn(x, row(ffn_norm_w[l]), w_gate_b[l], w_upp_b[l], ffn_conv_w[l][:, :D_FF], ffn_conv_w[l][:, D_FF:],
                     row(ffn_conv_b[l][:D_FF]), row(ffn_conv_b[l][D_FF:]), w_down_b[l], row(final_norm_w), tm,
                     final_norm=(l == depth - 1))
        return x

    return trunk(x_prompt), trunk(x_sample)
```

```python
import functools

import numpy as np
import jax
import jax.numpy as jnp
from jax import lax
from jax.experimental import pallas as pl
from jax.experimental.pallas import tpu as pltpu

F32 = jnp.float32
BF16 = jnp.bfloat16

LANE = 128
SUBLANE = 8
VMEM_LIMIT = 56 * 1024 * 1024

D_MODEL = 1024
RET_HEADS = 6
RET_HEAD_DIM = 64
RET_W = RET_HEADS * RET_HEAD_DIM
RET_CHUNK = 128
RET_PAIRS = RET_HEADS // 2
MLA_HEADS = 6
MLA_NOPE = 64
MLA_ROPE = 32
MLA_V = 64
MLA_Q_LORA = 384
MLA_KV_LORA = 128
MLA_W = MLA_HEADS * MLA_V
MLA_PAIRS = MLA_HEADS // 2
MLA_QK_W = MLA_HEADS * LANE
MLA_VT_ROWS = 80
MLA_VT_W = MLA_HEADS * MLA_VT_ROWS
CONV_CH = D_MODEL - RET_W - MLA_W
CONV_K = 31
CONV_HALO = 16
D_FF = 2816
FF_CHUNK = 256
FFN_HALO = 16
FFN_DOWN_GROUP = 4
CONV_ROWS = 64
ROPE_BASE = 10000.0
EPS = 1e-6

C_Q = 0
C_K = C_Q + RET_W
C_V = C_K + RET_W
C_G = C_V + RET_W
C_CQ = C_G + RET_W
C_CKV = C_CQ + MLA_Q_LORA
C_KR = C_CKV + MLA_KV_LORA
C_CONV = C_KR + LANE
IN_COLS_P = C_CONV + 2 * CONV_CH

NT_DIMS = (((1,), (1,)), ((), ()))
TN_DIMS = (((0,), (0,)), ((), ()))


def _params(*sem):
    return pltpu.CompilerParams(dimension_semantics=sem, vmem_limit_bytes=VMEM_LIMIT)


def _const_spec(shape):
    nd = len(shape)
    return pl.BlockSpec(shape, lambda *_: (0,) * nd)


def _rms(x, w):
    ms = jnp.mean(x * x, axis=-1, keepdims=True)
    return x * lax.rsqrt(ms + EPS) * w


def _sigmoid(x):
    return 1.0 / (1.0 + jnp.exp(-x))


def _rope_table_kernel(fr_ref, sr_ref, fm_ref, sm_ref, fc_ref, cr_ref, snr_ref, cm_ref, snm_ref, ct_ref, snt_ref, *, ts):
    pos = (pl.program_id(0) * ts + lax.broadcasted_iota(jnp.int32, (ts, LANE), 0)).astype(F32)
    ar = pos * fr_ref[...]
    cr_ref[...] = jnp.cos(ar)
    snr_ref[...] = jnp.sin(ar) * sr_ref[...]
    am = pos * fm_ref[...]
    cm_ref[...] = jnp.cos(am)
    snm_ref[...] = jnp.sin(am) * sm_ref[...]
    half_m = MLA_ROPE // 2
    pos_t = (pl.program_id(0) * ts + lax.broadcasted_iota(jnp.int32, (half_m, ts), 1)).astype(F32)
    at = fc_ref[...] * pos_t
    ct_ref[...] = jnp.cos(at)
    snt_ref[...] = jnp.sin(at)


def _rope_tables(S):
    c = np.arange(LANE)
    half_r = RET_HEAD_DIM // 2
    half_m = MLA_ROPE // 2
    freqs_r = 1.0 / (ROPE_BASE ** (jnp.arange(half_r, dtype=F32) / half_r))
    freqs_m = 1.0 / (ROPE_BASE ** (jnp.arange(half_m, dtype=F32) / half_m))
    fr = freqs_r[c % half_r][None, :]
    sr = jnp.asarray(np.where((c % RET_HEAD_DIM) < half_r, -1.0, 1.0), F32)[None, :]
    in_rope = (c >= MLA_NOPE) & (c < MLA_NOPE + MLA_ROPE)
    fm = jnp.where(in_rope, freqs_m[(c - MLA_NOPE) % half_m], 0.0)[None, :]
    sm = jnp.asarray(np.where(c < MLA_NOPE + half_m, -1.0, 1.0), F32)[None, :]
    fc = freqs_m[:, None]
    ts = min(S, 1024)
    row = _const_spec((1, LANE))
    tab = pl.BlockSpec((ts, LANE), lambda i: (i, 0))
    tab_t = pl.BlockSpec((half_m, ts), lambda i: (0, i))
    return pl.pallas_call(
        functools.partial(_rope_table_kernel, ts=ts),
        grid=(S // ts,),
        in_specs=[row, row, row, row, _const_spec((half_m, 1))],
        out_specs=[tab, tab, tab, tab, tab_t, tab_t],
        out_shape=[jax.ShapeDtypeStruct((S, LANE), F32)] * 4 + [jax.ShapeDtypeStruct((half_m, S), F32)] * 2,
        compiler_params=_params("parallel"),
    )(fr, sr, fm, sm, fc)


def _inproj_kernel(x_ref, nw_ref, w_ref, cr_ref, sr_ref,
                   q_ref, k_ref, v_ref, g_ref, cq_ref, ckv_ref, krp_ref, cin_ref, *, tm):
    h = _rms(x_ref[0], nw_ref[...]).astype(BF16)

    def seg(lo, width):
        return jnp.dot(h, w_ref[:, lo:lo + width], preferred_element_type=F32)

    cr = cr_ref[...]
    sr = sr_ref[...]
    lane = lax.broadcasted_iota(jnp.int32, (tm, LANE), 1)
    first_half = (lane % RET_HEAD_DIM) < (RET_HEAD_DIM // 2)

    def rope_block(xb):
        partner = jnp.where(first_half, pltpu.roll(xb, LANE - 32, 1), pltpu.roll(xb, 32, 1))
        return xb * cr + partner * sr

    for b in range(RET_W // LANE):
        sl = slice(b * LANE, (b + 1) * LANE)
        q_ref[0, :, sl] = rope_block(seg(C_Q + b * LANE, LANE)).astype(BF16)
        k_ref[0, :, sl] = (rope_block(seg(C_K + b * LANE, LANE)) * (RET_HEAD_DIM ** -0.5)).astype(BF16)
    v_ref[0] = seg(C_V, RET_W).astype(BF16)
    g_ref[0] = seg(C_G, RET_W)
    cq_ref[0] = seg(C_CQ, MLA_Q_LORA)
    ckv_ref[0] = seg(C_CKV, MLA_KV_LORA)
    krp_ref[0] = seg(C_KR, LANE)
    cin_ref[0] = seg(C_CONV, 2 * CONV_CH)


def _inproj(x, norm_w, w_in_p, cos_r, sin_r, tm):
    B, S, _ = x.shape

    def tok(width):
        return pl.BlockSpec((1, tm, width), lambda b, i: (b, i, 0))

    tab = pl.BlockSpec((tm, LANE), lambda b, i: (i, 0))
    widths = [RET_W, RET_W, RET_W, RET_W, MLA_Q_LORA, MLA_KV_LORA, LANE, 2 * CONV_CH]
    dtypes = [BF16, BF16, BF16, F32, F32, F32, F32, F32]
    return pl.pallas_call(
        functools.partial(_inproj_kernel, tm=tm),
        grid=(B, S // tm),
        in_specs=[tok(D_MODEL), _const_spec((1, D_MODEL)), _const_spec((D_MODEL, IN_COLS_P)), tab, tab],
        out_specs=[tok(w) for w in widths],
        out_shape=[jax.ShapeDtypeStruct((B, S, w), d) for w, d in zip(widths, dtypes)],
        compiler_params=_params("parallel", "parallel"),
    )(x, norm_w, w_in_p, cos_r, sin_r)


def _retention_tables():
    C = RET_CHUNK
    hh = np.arange(RET_HEADS, dtype=np.float64)
    gf = 1.0 - 2.0 ** (-5.0 - hh)
    gb = 1.0 - 2.0 ** (-5.5 - hh)
    idx = np.arange(C, dtype=np.float64)
    diff = idx[:, None] - idx[None, :]
    dcomb = np.zeros((RET_HEADS, C, C))
    for h in range(RET_HEADS):
        dcomb[h] = np.where(diff >= 0, gf[h] ** np.maximum(diff, 0), gb[h] ** np.maximum(-diff, 0))
    head_of_lane = np.arange(LANE) // RET_HEAD_DIM

    def per_pair_rows(fn):
        out = np.zeros((RET_PAIRS, C, LANE))
        for p in range(RET_PAIRS):
            for e in range(2):
                out[p][:, head_of_lane == e] = fn(2 * p + e)[:, None]
        return out

    qdf = per_pair_rows(lambda h: gf[h] ** (idx + 1.0))
    qdb = per_pair_rows(lambda h: gb[h] ** (C - idx))
    kdf = per_pair_rows(lambda h: gf[h] ** (C - 1.0 - idx))
    kdb = per_pair_rows(lambda h: gb[h] ** idx)
    same_head = head_of_lane[:, None] == head_of_lane[None, :]
    cdf = np.zeros((RET_PAIRS, LANE, LANE))
    cdb = np.zeros((RET_PAIRS, LANE, LANE))
    for p in range(RET_PAIRS):
        for e in range(2):
            blk = (head_of_lane[:, None] == e) & (head_of_lane[None, :] == e)
            cdf[p][blk] = gf[2 * p + e] ** C
            cdb[p][blk] = gb[2 * p + e] ** C
    arrs = dict(dcomb=dcomb, qdf=qdf, qdb=qdb, kdf=kdf, kdb=kdb, cdf=cdf, cdb=cdb, bm=same_head.astype(np.float64))
    return {name: jnp.asarray(a, F32) for name, a in arrs.items()}


def _kv_outer(kp, vp, kdecay, block_mask):
    kd = (kp.astype(F32) * kdecay).astype(BF16)
    return lax.dot_general(kd, vp, TN_DIMS, preferred_element_type=F32) * block_mask


def _retention_bwd_kernel(k_ref, v_ref, kdb_ref, cdb_ref, bm_ref, ub_ref, sb_ref, *, chunks):
    @pl.when(pl.program_id(1) == 0)
    def _():
        sb_ref[...] = jnp.zeros_like(sb_ref)

    C = RET_CHUNK
    for cc in reversed(range(chunks)):
        rows = slice(cc * C, (cc + 1) * C)
        for p in range(RET_PAIRS):
            cols = slice(p * LANE, (p + 1) * LANE)
            state = sb_ref[p]
            ub_ref[0, cc, p] = state.astype(BF16)
            sb_ref[p] = state * cdb_ref[p] + _kv_outer(k_ref[0, rows, cols], v_ref[0, rows, cols], kdb_ref[p],
                                                       bm_ref[...])


def _retention_bwd_states(k, v, tables, tm):
    B, S, _ = k.shape
    chunks = tm // RET_CHUNK
    T = S // tm
    kdb, cdb, bm = tables["kdb"], tables["cdb"], tables["bm"]
    tok = pl.BlockSpec((1, tm, RET_W), lambda b, i: (b, T - 1 - i, 0))
    return pl.pallas_call(
        functools.partial(_retention_bwd_kernel, chunks=chunks),
        grid=(B, T),
        in_specs=[tok, tok, _const_spec(kdb.shape), _const_spec(cdb.shape), _const_spec(bm.shape)],
        out_specs=pl.BlockSpec((1, chunks, RET_PAIRS, LANE, LANE), lambda b, i: (b, T - 1 - i, 0, 0, 0)),
        out_shape=jax.ShapeDtypeStruct((B, S // RET_CHUNK, RET_PAIRS, LANE, LANE), BF16),
        scratch_shapes=[pltpu.VMEM((RET_PAIRS, LANE, LANE), F32)],
        compiler_params=_params("parallel", "arbitrary"),
    )(k, v, kdb, cdb, bm)


def _retention_forward_tile(q_ref, k_ref, v_ref, g_ref, ub_ref, gnw_ref, dcomb_ref, qdf_ref, qdb_ref, kdf_ref,
                            cdf_ref, bm_ref, sf_ref, yr_ref, chunk_range):
    C = RET_CHUNK
    lane = lax.broadcasted_iota(jnp.int32, (C, LANE), 1)
    low = lane < RET_HEAD_DIM
    inv = 1.0 / RET_HEAD_DIM
    for cc in chunk_range:
        rows = slice(cc * C, (cc + 1) * C)
        for p in range(RET_PAIRS):
            cols = slice(p * LANE, (p + 1) * LANE)
            qp = q_ref[0, rows, cols]
            kp = k_ref[0, rows, cols]
            vp = v_ref[0, rows, cols]
            halves = []
            for e in range(2):
                qe = jnp.where(low if e == 0 else jnp.logical_not(low), qp, jnp.zeros_like(qp))
                s = lax.dot_general(qe, kp, NT_DIMS, preferred_element_type=F32)
                pe = (s * dcomb_ref[2 * p + e]).astype(BF16)
                halves.append(jnp.dot(pe, vp, preferred_element_type=F32))
            y = jnp.where(low, halves[0], halves[1])
            state = sf_ref[p]
            y = y + jnp.dot(qp, state.astype(BF16), preferred_element_type=F32) * qdf_ref[p]
            y = y + jnp.dot(qp, ub_ref[0, cc, p], preferred_element_type=F32) * qdb_ref[p]
            sf_ref[p] = state * cdf_ref[p] + _kv_outer(kp, vp, kdf_ref[p], bm_ref[...])

            mu_lo = jnp.sum(jnp.where(low, y, 0.0), axis=-1, keepdims=True) * inv
            mu_hi = jnp.sum(jnp.where(low, 0.0, y), axis=-1, keepdims=True) * inv
            d = y - jnp.where(low, mu_lo, mu_hi)
            d2 = d * d
            var_lo = jnp.sum(jnp.where(low, d2, 0.0), axis=-1, keepdims=True) * inv
            var_hi = jnp.sum(jnp.where(low, 0.0, d2), axis=-1, keepdims=True) * inv
            yn = d * lax.rsqrt(jnp.where(low, var_lo, var_hi) + EPS) * gnw_ref[:, cols]
            g = g_ref[0, rows, cols]
            yr_ref[rows, cols] = (g * _sigmoid(g) * yn).astype(BF16)


def _mla_prep_kernel(cq_ref, ckv_ref, krp_ref, cm_ref, sm_ref, ct_ref, st_ref, qnw_ref, kvnw_ref,
                     wqt_ref, wk_ref, wvt_ref, vone_ref, qt_ref, ko_ref, vt_ref, *, tm):
    cm = cm_ref[...]
    sm = sm_ref[...]
    lane = lax.broadcasted_iota(jnp.int32, (tm, LANE), 1)
    first_half = lane < MLA_NOPE + MLA_ROPE // 2
    partner_of = lambda xb: jnp.where(first_half, pltpu.roll(xb, LANE - 16, 1), pltpu.roll(xb, 16, 1))

    cqn = _rms(cq_ref[0], qnw_ref[...]).astype(BF16)
    ckvn = _rms(ckv_ref[0], kvnw_ref[...]).astype(BF16)
    krp = krp_ref[0]
    kr = krp * cm + partner_of(krp) * sm

    ct = ct_ref[...]
    st = st_ref[...]
    qscale = (MLA_NOPE + MLA_ROPE) ** -0.5 * np.log2(np.e)
    half = MLA_ROPE // 2
    r0 = MLA_NOPE
    for h in range(MLA_HEADS):
        base = h * LANE
        rows = slice(base, base + LANE)
        qt = lax.dot_general(wqt_ref[rows, :], cqn, NT_DIMS, preferred_element_type=F32)
        x1 = qt[r0:r0 + half]
        x2 = qt[r0 + half:r0 + 2 * half]
        qt_ref[0, base:base + r0, :] = (qt[0:r0] * qscale).astype(BF16)
        qt_ref[0, base + r0:base + r0 + half, :] = ((x1 * ct - x2 * st) * qscale).astype(BF16)
        qt_ref[0, base + r0 + half:base + r0 + 2 * half, :] = ((x2 * ct + x1 * st) * qscale).astype(BF16)
        qt_ref[0, base + r0 + 2 * half:base + LANE, :] = jnp.zeros((LANE - r0 - 2 * half, tm), BF16)
        ko_ref[0, :, rows] = (jnp.dot(ckvn, wk_ref[:, rows], preferred_element_type=F32) + kr).astype(BF16)
        vrows = slice(h * MLA_VT_ROWS, (h + 1) * MLA_VT_ROWS)
        vt = lax.dot_general(wvt_ref[vrows, :], ckvn, NT_DIMS, preferred_element_type=F32)
        vt_ref[0, vrows, :] = (vt + vone_ref[vrows, :]).astype(BF16)


def _mla_prep(cq, ckv, krp, cos_m, sin_m, cos_t, sin_t, qn_w, kvn_w, wqt_p, wk_p, wvt_p, vone, tm):
    B, S, _ = cq.shape

    def tok(width):
        return pl.BlockSpec((1, tm, width), lambda b, i: (b, i, 0))

    tab = pl.BlockSpec((tm, LANE), lambda b, i: (i, 0))
    tab_t = pl.BlockSpec((MLA_ROPE // 2, tm), lambda b, i: (0, i))
    feat = lambda rows: pl.BlockSpec((1, rows, tm), lambda b, i: (b, 0, i))
    return pl.pallas_call(
        functools.partial(_mla_prep_kernel, tm=tm),
        grid=(B, S // tm),
        in_specs=[tok(MLA_Q_LORA), tok(MLA_KV_LORA), tok(LANE), tab, tab, tab_t, tab_t,
                  _const_spec((1, MLA_Q_LORA)), _const_spec((1, MLA_KV_LORA)),
                  _const_spec((MLA_QK_W, MLA_Q_LORA)), _const_spec((MLA_KV_LORA, MLA_QK_W)),
                  _const_spec((MLA_VT_W, MLA_KV_LORA)), _const_spec((MLA_VT_W, 1))],
        out_specs=[feat(MLA_QK_W), tok(MLA_QK_W), feat(MLA_VT_W)],
        out_shape=[jax.ShapeDtypeStruct((B, MLA_QK_W, S), BF16), jax.ShapeDtypeStruct((B, S, MLA_QK_W), BF16),
                   jax.ShapeDtypeStruct((B, MLA_VT_W, S), BF16)],
        compiler_params=_params("parallel", "parallel"),
    )(cq, ckv, krp, cos_m, sin_m, cos_t, sin_t, qn_w, kvn_w, wqt_p, wk_p, wvt_p, vone)


def _attention_kernel(qt_ref, k_ref, vt_ref, o_ref, acc_ref, s_ref, *, tq, tk, seq):
    acc_ref[...] = jnp.zeros_like(acc_ref)
    qts = [qt_ref[0, e * LANE:(e + 1) * LANE, :] for e in range(2)]
    heads = [slice(e * LANE, (e + 1) * LANE) for e in range(2)]
    vheads = [slice(e * MLA_VT_ROWS, (e + 1) * MLA_VT_ROWS) for e in range(2)]
    n_trips = seq // (2 * tk)

    def scores(slot, j):
        off = pl.multiple_of(j * tk, tk)
        block_max = []
        for e in range(2):
            st = jnp.dot(k_ref[0, pl.ds(off, tk), heads[e]], qts[e], preferred_element_type=F32)
            s_ref[slot, e] = st
            block_max.append(jnp.max(st, axis=0, keepdims=True))
        return tuple(block_max)

    def softmax_pv(slot, j, ms, block_max):
        off = pl.multiple_of(j * tk, tk)
        new_ms = []
        for e in range(2):
            m_new = jnp.maximum(ms[e], block_max[e])
            alpha = jnp.exp2(ms[e] - m_new)
            pt = jnp.exp2(s_ref[slot, e] - m_new).astype(BF16)
            pv = jnp.dot(vt_ref[0, vheads[e], pl.ds(off, tk)], pt, preferred_element_type=F32)
            acc_ref[e] = alpha * acc_ref[e] + pv
            new_ms.append(m_new)
        return tuple(new_ms)

    def trip(t, carry, last):
        ms, bmax0 = carry
        j0 = 2 * t
        bmax1 = scores(1, j0 + 1)
        ms = softmax_pv(0, j0, ms, bmax0)
        if not last:
            bmax0 = scores(0, j0 + 2)
        return softmax_pv(1, j0 + 1, ms, bmax1), bmax0

    m0 = jnp.full((1, tq), -jnp.inf, F32)
    carry = ((m0, m0), scores(0, 0))
    carry = lax.fori_loop(0, n_trips - 1, lambda t, c: trip(t, c, False), carry)
    trip(n_trips - 1, carry, True)
    outs = []
    for e in range(2):
        acc = acc_ref[e]
        outs.append(acc[0:MLA_V] / acc[MLA_V:MLA_V + 1])
    o_ref[0] = jnp.concatenate(outs, axis=0).T.astype(BF16)


def _attention(qt, k, vt, tq, tk):
    B, S, _ = k.shape
    pw = 2 * LANE
    return pl.pallas_call(
        functools.partial(_attention_kernel, tq=tq, tk=tk, seq=S),
        grid=(B, MLA_PAIRS, S // tq),
        in_specs=[pl.BlockSpec((1, pw, tq), lambda b, p, i: (b, p, i)),
                  pl.BlockSpec((1, S, pw), lambda b, p, i: (b, 0, p)),
                  pl.BlockSpec((1, 2 * MLA_VT_ROWS, S), lambda b, p, i: (b, p, 0))],
        out_specs=pl.BlockSpec((1, tq, LANE), lambda b, p, i: (b, i, p)),
        out_shape=jax.ShapeDtypeStruct((B, S, MLA_W), BF16),
        scratch_shapes=[pltpu.VMEM((2, MLA_VT_ROWS, tq), F32), pltpu.VMEM((2, 2, tk, tq), F32)],
        compiler_params=_params("parallel", "parallel", "parallel"),
    )(qt, k, vt)


def _conv_prepare(prev_ref, cur_ref, next_ref, ext_ref, sh_ref, first, last, tm):
    def glu(z):
        return z[:, :CONV_CH] * _sigmoid(z[:, CONV_CH:])

    ext_ref[0:CONV_HALO] = jnp.where(first, 0.0, glu(prev_ref[0]))
    ext_ref[CONV_HALO:CONV_HALO + tm] = glu(cur_ref[0])
    ext_ref[CONV_HALO + tm:2 * CONV_HALO + tm] = jnp.where(last, 0.0, glu(next_ref[0]))
    span = tm + 2 * CONV_HALO - SUBLANE
    for j in range(SUBLANE - 1):
        sh_ref[j] = ext_ref[pl.ds(j + 1, span), :]


def _conv_rows(row_starts, rows, ext_ref, sh_ref, dww_ref, dwb_ref, lnw_ref, lnb_ref, pww_ref, pwb_ref, yc_ref):
    base = CONV_HALO - CONV_K // 2
    for r in row_starts:
        acc = jnp.zeros((rows, CONV_CH), F32) + dwb_ref[...]
        for t in range(CONV_K):
            a, j = divmod(base + t, SUBLANE)
            src = ext_ref[pl.ds(r + a * SUBLANE, rows), :] if j == 0 else sh_ref[j - 1, pl.ds(r + a * SUBLANE, rows), :]
            acc = acc + dww_ref[t:t + 1, :] * src
        mu = jnp.mean(acc, axis=-1, keepdims=True)
        d = acc - mu
        var = jnp.mean(d * d, axis=-1, keepdims=True)
        c = d * lax.rsqrt(var + EPS) * lnw_ref[...] + lnb_ref[...]
        c = (c * _sigmoid(c)).astype(BF16)
        y = jnp.dot(c, pww_ref[...], preferred_element_type=F32) + pwb_ref[...]
        yc_ref[r:r + rows, :] = y.astype(BF16)


def _mixer_tail_kernel(x_ref, q_ref, k_ref, v_ref, g_ref, ub_ref, ym_ref, prev_ref, cur_ref, next_ref,
                       gnw_ref, dcomb_ref, qdf_ref, qdb_ref, kdf_ref, cdf_ref, bm_ref,
                       dww_ref, dwb_ref, lnw_ref, lnb_ref, pww_ref, pwb_ref, wo_ref,
                       o_ref, sf_ref, ext_ref, sh_ref, yr_ref, yc_ref, *, tm, n_tiles):
    i = pl.program_id(1)

    @pl.when(i == 0)
    def _():
        sf_ref[...] = jnp.zeros_like(sf_ref)

    _conv_prepare(prev_ref, cur_ref, next_ref, ext_ref, sh_ref, i == 0, i == n_tiles - 1, tm)
    _retention_forward_tile(q_ref, k_ref, v_ref, g_ref, ub_ref, gnw_ref, dcomb_ref, qdf_ref, qdb_ref, kdf_ref,
                            cdf_ref, bm_ref, sf_ref, yr_ref, range(tm // RET_CHUNK))
    _conv_rows(range(0, tm, CONV_ROWS), CONV_ROWS, ext_ref, sh_ref, dww_ref, dwb_ref, lnw_ref, lnb_ref, pww_ref,
               pwb_ref, yc_ref)
    acc = x_ref[0]
    acc = acc + jnp.dot(yr_ref[...], wo_ref[0:RET_W, :], preferred_element_type=F32)
    acc = acc + jnp.dot(ym_ref[0], wo_ref[RET_W:RET_W + MLA_W, :], preferred_element_type=F32)
    acc = acc + jnp.dot(yc_ref[...], wo_ref[RET_W + MLA_W:, :], preferred_element_type=F32)
    o_ref[0] = acc


def _mixer_tail(x, q, k, v, g, ub, y_mla, cin, gn_w, tables, dw_w, dw_b, ln_w, ln_b, pw_w, pw_b, wo, tm):
    B, S, _ = x.shape
    n_tiles = S // tm
    chunks = tm // RET_CHUNK
    hb = tm // CONV_HALO
    last = S // CONV_HALO - 1
    cw = 2 * CONV_CH

    def tok(width):
        return pl.BlockSpec((1, tm, width), lambda b, i: (b, i, 0))

    row = _const_spec((1, CONV_CH))
    ret_tabs = [tables[n] for n in ("dcomb", "qdf", "qdb", "kdf", "cdf", "bm")]
    return pl.pallas_call(
        functools.partial(_mixer_tail_kernel, tm=tm, n_tiles=n_tiles),
        grid=(B, n_tiles),
        in_specs=[tok(D_MODEL), tok(RET_W), tok(RET_W), tok(RET_W), tok(RET_W),
                  pl.BlockSpec((1, chunks, RET_PAIRS, LANE, LANE), lambda b, i: (b, i, 0, 0, 0)),
                  tok(MLA_W),
                  pl.BlockSpec((1, CONV_HALO, cw), lambda b, i: (b, jnp.maximum(i * hb - 1, 0), 0)),
                  tok(cw),
                  pl.BlockSpec((1, CONV_HALO, cw), lambda b, i: (b, jnp.minimum((i + 1) * hb, last), 0)),
                  _const_spec((1, RET_W))] + [_const_spec(t.shape) for t in ret_tabs] + [
                  _const_spec((CONV_K, CONV_CH)), row, row, row, _const_spec((CONV_CH, CONV_CH)), row,
                  _const_spec((D_MODEL, D_MODEL))],
        out_specs=tok(D_MODEL),
        out_shape=jax.ShapeDtypeStruct((B, S, D_MODEL), F32),
        scratch_shapes=[pltpu.VMEM((RET_PAIRS, LANE, LANE), F32),
                        pltpu.VMEM((tm + 2 * CONV_HALO, CONV_CH), F32),
                        pltpu.VMEM((SUBLANE - 1, tm + 2 * CONV_HALO - SUBLANE, CONV_CH), F32),
                        pltpu.VMEM((tm, RET_W), BF16),
                        pltpu.VMEM((tm, CONV_CH), BF16)],
        compiler_params=_params("parallel", "arbitrary"),
    )(x, q, k, v, g, ub, y_mla, cin, cin, cin, gn_w, *ret_tabs, dw_w, dw_b, ln_w, ln_b, pw_w, pw_b, wo)


def _ffn_kernel(prev_ref, cur_ref, next_ref, nw_ref, wg_ref, wu_ref, cwg_ref, cwu_ref, cbg_ref, cbu_ref, wd_ref,
                fw_ref, o_ref, h_ref, act_ref, *, tm, n_tiles, final_norm):
    i = pl.program_id(1)
    nw = nw_ref[...]
    h_ref[0:FFN_HALO] = jnp.where(i > 0, _rms(prev_ref[0], nw), 0.0).astype(BF16)
    h_ref[FFN_HALO:FFN_HALO + tm] = _rms(cur_ref[0], nw).astype(BF16)
    h_ref[FFN_HALO + tm:2 * FFN_HALO + tm] = jnp.where(i < n_tiles - 1, _rms(next_ref[0], nw), 0.0).astype(BF16)

    def conv3(u, w_ref, b_ref, sl):
        w = w_ref[:, sl]
        return (w[0:1] * u[FFN_HALO - 1:FFN_HALO - 1 + tm] + w[1:2] * u[FFN_HALO:FFN_HALO + tm]
                + w[2:3] * u[FFN_HALO + 1:FFN_HALO + 1 + tm] + b_ref[:, sl])

    n_chunks = D_FF // FF_CHUNK
    acc = cur_ref[0]
    group_start = 0
    for c in range(n_chunks):
        sl = slice(c * FF_CHUNK, (c + 1) * FF_CHUNK)
        h = h_ref[...]
        gate = conv3(jnp.dot(h, wg_ref[:, sl], preferred_element_type=F32), cwg_ref, cbg_ref, sl)
        up = conv3(jnp.dot(h, wu_ref[:, sl], preferred_element_type=F32), cwu_ref, cbu_ref, sl)
        act_ref[:, sl] = (gate * _sigmoid(gate) * up).astype(BF16)
        if (c + 1) % FFN_DOWN_GROUP == 0 or c == n_chunks - 1:
            ksl = slice(group_start * FF_CHUNK, (c + 1) * FF_CHUNK)
            acc = acc + jnp.dot(act_ref[:, ksl], wd_ref[ksl, :], preferred_element_type=F32)
            group_start = c + 1
    if final_norm:
        acc = _rms(acc, fw_ref[...])
    o_ref[0] = acc


def _ffn(x, norm_w, wg, wu, cwg, cwu, cbg, cbu, wd, final_w, tm, final_norm):
    B, S, _ = x.shape
    n_tiles = S // tm
    hb = tm // FFN_HALO
    last = S // FFN_HALO - 1
    single = pl.Buffered(1)

    def resident(shape):
        return pl.BlockSpec(shape, lambda b, i: (0,) * len(shape), pipeline_mode=single)

    return pl.pallas_call(
        functools.partial(_ffn_kernel, tm=tm, n_tiles=n_tiles, final_norm=final_norm),
        grid=(B, n_tiles),
        in_specs=[pl.BlockSpec((1, FFN_HALO, D_MODEL), lambda b, i: (b, jnp.maximum(i * hb - 1, 0), 0)),
                  pl.BlockSpec((1, tm, D_MODEL), lambda b, i: (b, i, 0)),
                  pl.BlockSpec((1, FFN_HALO, D_MODEL), lambda b, i: (b, jnp.minimum((i + 1) * hb, last), 0)),
                  _const_spec((1, D_MODEL)),
                  resident((D_MODEL, D_FF)), resident((D_MODEL, D_FF)),
                  _const_spec((3, D_FF)), _const_spec((3, D_FF)), _const_spec((1, D_FF)), _const_spec((1, D_FF)),
                  resident((D_FF, D_MODEL)), _const_spec((1, D_MODEL))],
        out_specs=pl.BlockSpec((1, tm, D_MODEL), lambda b, i: (b, i, 0)),
        out_shape=jax.ShapeDtypeStruct((B, S, D_MODEL), F32),
        scratch_shapes=[pltpu.VMEM((tm + 2 * FFN_HALO, D_MODEL), BF16), pltpu.VMEM((tm, D_FF), BF16)],
        compiler_params=_params("parallel", "parallel"),
    )(x, x, x, norm_w, wg, wu, cwg, cwu, cbg, cbu, wd, final_w)


def _prep_layer_weights(w_in, mla_w_uq, mla_w_ukv):
    L = w_in.shape[0]
    c0 = 4 * RET_W
    c1 = c0 + MLA_Q_LORA
    c2 = c1 + MLA_KV_LORA
    c3 = c2 + MLA_ROPE
    z = lambda n: jnp.zeros((L, D_MODEL, n), w_in.dtype)
    w_in_p = jnp.concatenate([w_in[..., :c2], z(MLA_NOPE), w_in[..., c2:c3], z(LANE - MLA_NOPE - MLA_ROPE),
                              w_in[..., c3:]], axis=-1).astype(BF16)

    uq = mla_w_uq.reshape(L, MLA_Q_LORA, MLA_HEADS, MLA_NOPE + MLA_ROPE)
    wq_p = jnp.concatenate([uq, jnp.zeros((L, MLA_Q_LORA, MLA_HEADS, LANE - MLA_NOPE - MLA_ROPE), uq.dtype)],
                           axis=-1).reshape(L, MLA_Q_LORA, MLA_QK_W)
    wqt_p = jnp.swapaxes(wq_p, 1, 2).astype(BF16)

    ukv = mla_w_ukv.reshape(L, MLA_KV_LORA, MLA_HEADS, MLA_NOPE + MLA_V)
    kn = ukv[..., :MLA_NOPE]
    vm = ukv[..., MLA_NOPE:]
    wk_p = jnp.concatenate([kn, jnp.zeros_like(kn)], axis=-1).reshape(L, MLA_KV_LORA, MLA_QK_W).astype(BF16)
    v_pad = jnp.zeros((L, MLA_KV_LORA, MLA_HEADS, MLA_VT_ROWS - MLA_V), vm.dtype)
    wv_p = jnp.concatenate([vm, v_pad], axis=-1).reshape(L, MLA_KV_LORA, MLA_VT_W)
    wvt_p = jnp.swapaxes(wv_p, 1, 2).astype(BF16)

    vone = np.zeros((MLA_HEADS, MLA_VT_ROWS, 1), np.float32)
    vone[:, MLA_V, 0] = 1.0
    return w_in_p, wqt_p, wk_p, wvt_p, jnp.asarray(vone.reshape(MLA_VT_W, 1))


def _tile(S, want):
    return min(S, want)


def kernel(x_prompt, x_sample, attn_norm_w, w_in, ret_gn_w, mla_q_norm_w, mla_w_uq, mla_kv_norm_w, mla_w_ukv,
           conv_dw_w, conv_dw_b, conv_ln_w, conv_ln_b, conv_pw_w, conv_pw_b, w_out, ffn_norm_w, w_up, ffn_conv_w,
           ffn_conv_b, w_down, final_norm_w):
    depth = w_in.shape[0]
    w_in_p, wqt_p, wk_p, wvt_p, vone = _prep_layer_weights(w_in, mla_w_uq, mla_w_ukv)
    w_out_b = w_out.astype(BF16)
    w_gate_b = w_up[..., :D_FF].astype(BF16)
    w_upp_b = w_up[..., D_FF:].astype(BF16)
    w_down_b = w_down.astype(BF16)
    conv_pw_b16 = conv_pw_w.astype(BF16)
    ret_tables = _retention_tables()
    row = lambda a: a[None, :]

    def trunk(x):
        S = x.shape[1]
        cos_r, sin_r, cos_m, sin_m, cos_t, sin_t = _rope_tables(S)
        tm = _tile(S, 512)
        for l in range(depth):
            q, k, v, g, cq, ckv, krp, cin = _inproj(x, row(attn_norm_w[l]), w_in_p[l], cos_r, sin_r, tm)
            ub = _retention_bwd_states(k, v, ret_tables, tm)
            qh, kh, vh = _mla_prep(cq, ckv, krp, cos_m, sin_m, cos_t, sin_t, row(mla_q_norm_w[l]),
                                   row(mla_kv_norm_w[l]), wqt_p[l], wk_p[l], wvt_p[l], vone, tm)
            y_mla = _attention(qh, kh, vh, _tile(S, 512), _tile(S, 1024))
            x = _mixer_tail(x, q, k, v, g, ub, y_mla, cin, row(ret_gn_w[l]), ret_tables, conv_dw_w[l],
                            row(conv_dw_b[l]), row(conv_ln_w[l]), row(conv_ln_b[l]), conv_pw_b16[l],
                            row(conv_pw_b[l]), w_out_b[l], tm)
            x = _ffn(x, row(ffn_norm_w[l]), w_gate_b[l], w_upp_b[l], ffn_conv_w[l][:, :D_FF], ffn_conv_w[l][:, D_FF:],
                     row(ffn_conv_b[l][:D_FF]), row(ffn_conv_b[l][D_FF:]), w_down_b[l], row(final_norm_w), tm,
                     final_norm=(l == depth - 1))
        return x

    return trunk(x_prompt), trunk(x_sample)
```

```python
import functools

import numpy as np
import jax
import jax.numpy as jnp
from jax import lax
from jax.experimental import pallas as pl
from jax.experimental.pallas import tpu as pltpu

F32 = jnp.float32
BF16 = jnp.bfloat16

LANE = 128
SUBLANE = 8
VMEM_LIMIT = 56 * 1024 * 1024

D_MODEL = 1024
RET_HEADS = 6
RET_HEAD_DIM = 64
RET_W = RET_HEADS * RET_HEAD_DIM
RET_CHUNK = 128
RET_PAIRS = RET_HEADS // 2
MLA_HEADS = 6
MLA_NOPE = 64
MLA_ROPE = 32
MLA_V = 64
MLA_Q_LORA = 384
MLA_KV_LORA = 128
MLA_W = MLA_HEADS * MLA_V
MLA_PAIRS = MLA_HEADS // 2
MLA_QK_W = MLA_HEADS * LANE
MLA_VT_ROWS = 80
MLA_VT_W = MLA_HEADS * MLA_VT_ROWS
CONV_CH = D_MODEL - RET_W - MLA_W
CONV_K = 31
CONV_HALO = 16
D_FF = 2816
FF_CHUNK = 256
FFN_HALO = 16
FFN_DOWN_BOUNDS = (4, 8, 11)
ATT_AHEAD = 2
ATT_SLOTS = 4
CONV_ROWS = 64
ROPE_BASE = 10000.0
EPS = 1e-6

C_Q = 0
C_K = C_Q + RET_W
C_V = C_K + RET_W
C_G = C_V + RET_W
C_CQ = C_G + RET_W
C_CKV = C_CQ + MLA_Q_LORA
C_KR = C_CKV + MLA_KV_LORA
C_CONV = C_KR + LANE
IN_COLS_P = C_CONV + 2 * CONV_CH

NT_DIMS = (((1,), (1,)), ((), ()))
TN_DIMS = (((0,), (0,)), ((), ()))


def _params(*sem):
    return pltpu.CompilerParams(dimension_semantics=sem, vmem_limit_bytes=VMEM_LIMIT)


def _const_spec(shape):
    nd = len(shape)
    return pl.BlockSpec(shape, lambda *_: (0,) * nd)


def _rms(x, w):
    ms = jnp.mean(x * x, axis=-1, keepdims=True)
    return x * lax.rsqrt(ms + EPS) * w


def _sigmoid(x):
    return 1.0 / (1.0 + jnp.exp(-x))


def _rope_table_kernel(fr_ref, sr_ref, fm_ref, sm_ref, fc_ref, cr_ref, snr_ref, cm_ref, snm_ref, ct_ref, snt_ref, *, ts):
    pos = (pl.program_id(0) * ts + lax.broadcasted_iota(jnp.int32, (ts, LANE), 0)).astype(F32)
    ar = pos * fr_ref[...]
    cr_ref[...] = jnp.cos(ar)
    snr_ref[...] = jnp.sin(ar) * sr_ref[...]
    am = pos * fm_ref[...]
    cm_ref[...] = jnp.cos(am)
    snm_ref[...] = jnp.sin(am) * sm_ref[...]
    half_m = MLA_ROPE // 2
    pos_t = (pl.program_id(0) * ts + lax.broadcasted_iota(jnp.int32, (half_m, ts), 1)).astype(F32)
    at = fc_ref[...] * pos_t
    ct_ref[...] = jnp.cos(at)
    snt_ref[...] = jnp.sin(at)


def _rope_tables(S):
    c = np.arange(LANE)
    half_r = RET_HEAD_DIM // 2
    half_m = MLA_ROPE // 2
    freqs_r = 1.0 / (ROPE_BASE ** (jnp.arange(half_r, dtype=F32) / half_r))
    freqs_m = 1.0 / (ROPE_BASE ** (jnp.arange(half_m, dtype=F32) / half_m))
    fr = freqs_r[c % half_r][None, :]
    sr = jnp.asarray(np.where((c % RET_HEAD_DIM) < half_r, -1.0, 1.0), F32)[None, :]
    in_rope = (c >= MLA_NOPE) & (c < MLA_NOPE + MLA_ROPE)
    fm = jnp.where(in_rope, freqs_m[(c - MLA_NOPE) % half_m], 0.0)[None, :]
    sm = jnp.asarray(np.where(c < MLA_NOPE + half_m, -1.0, 1.0), F32)[None, :]
    fc = freqs_m[:, None]
    ts = min(S, 1024)
    row = _const_spec((1, LANE))
    tab = pl.BlockSpec((ts, LANE), lambda i: (i, 0))
    tab_t = pl.BlockSpec((half_m, ts), lambda i: (0, i))
    return pl.pallas_call(
        functools.partial(_rope_table_kernel, ts=ts),
        grid=(S // ts,),
        in_specs=[row, row, row, row, _const_spec((half_m, 1))],
        out_specs=[tab, tab, tab, tab, tab_t, tab_t],
        out_shape=[jax.ShapeDtypeStruct((S, LANE), F32)] * 4 + [jax.ShapeDtypeStruct((half_m, S), F32)] * 2,
        compiler_params=_params("parallel"),
    )(fr, sr, fm, sm, fc)


def _inproj_kernel(x_ref, nw_ref, w_ref, cr_ref, sr_ref,
                   q_ref, k_ref, v_ref, g_ref, cq_ref, ckv_ref, krp_ref, cin_ref, *, tm):
    h = _rms(x_ref[0], nw_ref[...]).astype(BF16)

    def seg(lo, width):
        return jnp.dot(h, w_ref[:, lo:lo + width], preferred_element_type=F32)

    cr = cr_ref[...]
    sr = sr_ref[...]
    lane = lax.broadcasted_iota(jnp.int32, (tm, LANE), 1)
    first_half = (lane % RET_HEAD_DIM) < (RET_HEAD_DIM // 2)

    def rope_block(xb):
        partner = jnp.where(first_half, pltpu.roll(xb, LANE - 32, 1), pltpu.roll(xb, 32, 1))
        return xb * cr + partner * sr

    qk = seg(C_Q, 2 * RET_W)
    for b in range(RET_W // LANE):
        sl = slice(b * LANE, (b + 1) * LANE)
        q_ref[0, :, sl] = rope_block(qk[:, sl]).astype(BF16)
        k_ref[0, :, sl] = (rope_block(qk[:, RET_W + b * LANE:RET_W + (b + 1) * LANE])
                           * (RET_HEAD_DIM ** -0.5)).astype(BF16)
    vg = seg(C_V, 2 * RET_W)
    v_ref[0] = vg[:, :RET_W].astype(BF16)
    g_ref[0] = vg[:, RET_W:]
    lat = seg(C_CQ, MLA_Q_LORA + MLA_KV_LORA)
    cq_ref[0] = lat[:, :MLA_Q_LORA]
    ckv_ref[0] = lat[:, MLA_Q_LORA:]
    rest = seg(C_KR, LANE + 2 * CONV_CH)
    krp_ref[0] = rest[:, :LANE]
    cin_ref[0] = rest[:, LANE:]


def _inproj(x, norm_w, w_in_p, cos_r, sin_r, tm):
    B, S, _ = x.shape

    def tok(width):
        return pl.BlockSpec((1, tm, width), lambda b, i: (b, i, 0))

    tab = pl.BlockSpec((tm, LANE), lambda b, i: (i, 0))
    widths = [RET_W, RET_W, RET_W, RET_W, MLA_Q_LORA, MLA_KV_LORA, LANE, 2 * CONV_CH]
    dtypes = [BF16, BF16, BF16, F32, F32, F32, F32, F32]
    return pl.pallas_call(
        functools.partial(_inproj_kernel, tm=tm),
        grid=(B, S // tm),
        in_specs=[tok(D_MODEL), _const_spec((1, D_MODEL)), _const_spec((D_MODEL, IN_COLS_P)), tab, tab],
        out_specs=[tok(w) for w in widths],
        out_shape=[jax.ShapeDtypeStruct((B, S, w), d) for w, d in zip(widths, dtypes)],
        compiler_params=_params("parallel", "parallel"),
    )(x, norm_w, w_in_p, cos_r, sin_r)


def _retention_tables():
    C = RET_CHUNK
    hh = np.arange(RET_HEADS, dtype=np.float64)
    gf = 1.0 - 2.0 ** (-5.0 - hh)
    gb = 1.0 - 2.0 ** (-5.5 - hh)
    idx = np.arange(C, dtype=np.float64)
    diff = idx[:, None] - idx[None, :]
    dcomb = np.zeros((RET_HEADS, C, C))
    for h in range(RET_HEADS):
        dcomb[h] = np.where(diff >= 0, gf[h] ** np.maximum(diff, 0), gb[h] ** np.maximum(-diff, 0))
    head_of_lane = np.arange(LANE) // RET_HEAD_DIM

    def per_pair_rows(fn):
        out = np.zeros((RET_PAIRS, C, LANE))
        for p in range(RET_PAIRS):
            for e in range(2):
                out[p][:, head_of_lane == e] = fn(2 * p + e)[:, None]
        return out

    qdf = per_pair_rows(lambda h: gf[h] ** (idx + 1.0))
    qdb = per_pair_rows(lambda h: gb[h] ** (C - idx))
    kdf = per_pair_rows(lambda h: gf[h] ** (C - 1.0 - idx))
    kdb = per_pair_rows(lambda h: gb[h] ** idx)
    same_head = head_of_lane[:, None] == head_of_lane[None, :]
    cdf = np.zeros((RET_PAIRS, LANE, LANE))
    cdb = np.zeros((RET_PAIRS, LANE, LANE))
    for p in range(RET_PAIRS):
        for e in range(2):
            blk = (head_of_lane[:, None] == e) & (head_of_lane[None, :] == e)
            cdf[p][blk] = gf[2 * p + e] ** C
            cdb[p][blk] = gb[2 * p + e] ** C
    arrs = dict(dcomb=dcomb, qdf=qdf, qdb=qdb, kdf=kdf, kdb=kdb, cdf=cdf, cdb=cdb, bm=same_head.astype(np.float64))
    return {name: jnp.asarray(a, F32) for name, a in arrs.items()}


def _kv_outer(kp, vp, kdecay, block_mask):
    kd = (kp.astype(F32) * kdecay).astype(BF16)
    return lax.dot_general(kd, vp, TN_DIMS, preferred_element_type=F32) * block_mask


def _retention_bwd_kernel(k_ref, v_ref, kdb_ref, cdb_ref, bm_ref, ub_ref, sb_ref, *, chunks):
    @pl.when(pl.program_id(1) == 0)
    def _():
        sb_ref[...] = jnp.zeros_like(sb_ref)

    C = RET_CHUNK
    for cc in reversed(range(chunks)):
        rows = slice(cc * C, (cc + 1) * C)
        for p in range(RET_PAIRS):
            cols = slice(p * LANE, (p + 1) * LANE)
            state = sb_ref[p]
            ub_ref[0, cc, p] = state.astype(BF16)
            sb_ref[p] = state * cdb_ref[p] + _kv_outer(k_ref[0, rows, cols], v_ref[0, rows, cols], kdb_ref[p],
                                                       bm_ref[...])


def _retention_bwd_states(k, v, tables, tm):
    B, S, _ = k.shape
    chunks = tm // RET_CHUNK
    T = S // tm
    kdb, cdb, bm = tables["kdb"], tables["cdb"], tables["bm"]
    tok = pl.BlockSpec((1, tm, RET_W), lambda b, i: (b, T - 1 - i, 0))
    return pl.pallas_call(
        functools.partial(_retention_bwd_kernel, chunks=chunks),
        grid=(B, T),
        in_specs=[tok, tok, _const_spec(kdb.shape), _const_spec(cdb.shape), _const_spec(bm.shape)],
        out_specs=pl.BlockSpec((1, chunks, RET_PAIRS, LANE, LANE), lambda b, i: (b, T - 1 - i, 0, 0, 0)),
        out_shape=jax.ShapeDtypeStruct((B, S // RET_CHUNK, RET_PAIRS, LANE, LANE), BF16),
        scratch_shapes=[pltpu.VMEM((RET_PAIRS, LANE, LANE), F32)],
        compiler_params=_params("parallel", "arbitrary"),
    )(k, v, kdb, cdb, bm)


def _retention_forward_tile(q_ref, k_ref, v_ref, g_ref, ub_ref, gnw_ref, dcomb_ref, qdf_ref, qdb_ref, kdf_ref,
                            cdf_ref, bm_ref, sf_ref, yr_ref, chunk_range):
    C = RET_CHUNK
    lane = lax.broadcasted_iota(jnp.int32, (C, LANE), 1)
    low = lane < RET_HEAD_DIM
    inv = 1.0 / RET_HEAD_DIM
    for cc in chunk_range:
        rows = slice(cc * C, (cc + 1) * C)
        for p in range(RET_PAIRS):
            cols = slice(p * LANE, (p + 1) * LANE)
            qp = q_ref[0, rows, cols]
            kp = k_ref[0, rows, cols]
            vp = v_ref[0, rows, cols]
            halves = []
            for e in range(2):
                qe = jnp.where(low if e == 0 else jnp.logical_not(low), qp, jnp.zeros_like(qp))
                s = lax.dot_general(qe, kp, NT_DIMS, preferred_element_type=F32)
                pe = (s * dcomb_ref[2 * p + e]).astype(BF16)
                halves.append(jnp.dot(pe, vp, preferred_element_type=F32))
            y = jnp.where(low, halves[0], halves[1])
            state = sf_ref[p]
            y = y + jnp.dot(qp, state.astype(BF16), preferred_element_type=F32) * qdf_ref[p]
            y = y + jnp.dot(qp, ub_ref[0, cc, p], preferred_element_type=F32) * qdb_ref[p]
            sf_ref[p] = state * cdf_ref[p] + _kv_outer(kp, vp, kdf_ref[p], bm_ref[...])

            mu_lo = jnp.sum(jnp.where(low, y, 0.0), axis=-1, keepdims=True) * inv
            mu_hi = jnp.sum(jnp.where(low, 0.0, y), axis=-1, keepdims=True) * inv
            d = y - jnp.where(low, mu_lo, mu_hi)
            d2 = d * d
            var_lo = jnp.sum(jnp.where(low, d2, 0.0), axis=-1, keepdims=True) * inv
            var_hi = jnp.sum(jnp.where(low, 0.0, d2), axis=-1, keepdims=True) * inv
            yn = d * lax.rsqrt(jnp.where(low, var_lo, var_hi) + EPS) * gnw_ref[:, cols]
            g = g_ref[0, rows, cols]
            yr_ref[rows, cols] = (g * _sigmoid(g) * yn).astype(BF16)


def _mla_prep_kernel(cq_ref, ckv_ref, krp_ref, cm_ref, sm_ref, ct_ref, st_ref, qnw_ref, kvnw_ref,
                     wqt_ref, wk_ref, wvt_ref, vone_ref, qt_ref, ko_ref, vt_ref, *, tm):
    cm = cm_ref[...]
    sm = sm_ref[...]
    lane = lax.broadcasted_iota(jnp.int32, (tm, LANE), 1)
    first_half = lane < MLA_NOPE + MLA_ROPE // 2
    partner_of = lambda xb: jnp.where(first_half, pltpu.roll(xb, LANE - 16, 1), pltpu.roll(xb, 16, 1))

    cqn = _rms(cq_ref[0], qnw_ref[...]).astype(BF16)
    ckvn = _rms(ckv_ref[0], kvnw_ref[...]).astype(BF16)
    krp = krp_ref[0]
    kr = krp * cm + partner_of(krp) * sm

    ct = ct_ref[...]
    st = st_ref[...]
    qscale = (MLA_NOPE + MLA_ROPE) ** -0.5 * np.log2(np.e)
    half = MLA_ROPE // 2
    r0 = MLA_NOPE
    for h in range(MLA_HEADS):
        base = h * LANE
        rows = slice(base, base + LANE)
        qt = lax.dot_general(wqt_ref[rows, :], cqn, NT_DIMS, preferred_element_type=F32)
        x1 = qt[r0:r0 + half]
        x2 = qt[r0 + half:r0 + 2 * half]
        qt_ref[0, base:base + r0, :] = (qt[0:r0] * qscale).astype(BF16)
        qt_ref[0, base + r0:base + r0 + half, :] = ((x1 * ct - x2 * st) * qscale).astype(BF16)
        qt_ref[0, base + r0 + half:base + r0 + 2 * half, :] = ((x2 * ct + x1 * st) * qscale).astype(BF16)
        qt_ref[0, base + r0 + 2 * half:base + LANE, :] = jnp.zeros((LANE - r0 - 2 * half, tm), BF16)
        ko_ref[0, :, rows] = (jnp.dot(ckvn, wk_ref[:, rows], preferred_element_type=F32) + kr).astype(BF16)
        vrows = slice(h * MLA_VT_ROWS, (h + 1) * MLA_VT_ROWS)
        vt = lax.dot_general(wvt_ref[vrows, :], ckvn, NT_DIMS, preferred_element_type=F32)
        vt_ref[0, vrows, :] = (vt + vone_ref[vrows, :]).astype(BF16)


def _mla_prep(cq, ckv, krp, cos_m, sin_m, cos_t, sin_t, qn_w, kvn_w, wqt_p, wk_p, wvt_p, vone, tm):
    B, S, _ = cq.shape

    def tok(width):
        return pl.BlockSpec((1, tm, width), lambda b, i: (b, i, 0))

    tab = pl.BlockSpec((tm, LANE), lambda b, i: (i, 0))
    tab_t = pl.BlockSpec((MLA_ROPE // 2, tm), lambda b, i: (0, i))
    feat = lambda rows: pl.BlockSpec((1, rows, tm), lambda b, i: (b, 0, i))
    return pl.pallas_call(
        functools.partial(_mla_prep_kernel, tm=tm),
        grid=(B, S // tm),
        in_specs=[tok(MLA_Q_LORA), tok(MLA_KV_LORA), tok(LANE), tab, tab, tab_t, tab_t,
                  _const_spec((1, MLA_Q_LORA)), _const_spec((1, MLA_KV_LORA)),
                  _const_spec((MLA_QK_W, MLA_Q_LORA)), _const_spec((MLA_KV_LORA, MLA_QK_W)),
                  _const_spec((MLA_VT_W, MLA_KV_LORA)), _const_spec((MLA_VT_W, 1))],
        out_specs=[feat(MLA_QK_W), tok(MLA_QK_W), feat(MLA_VT_W)],
        out_shape=[jax.ShapeDtypeStruct((B, MLA_QK_W, S), BF16), jax.ShapeDtypeStruct((B, S, MLA_QK_W), BF16),
                   jax.ShapeDtypeStruct((B, MLA_VT_W, S), BF16)],
        compiler_params=_params("parallel", "parallel"),
    )(cq, ckv, krp, cos_m, sin_m, cos_t, sin_t, qn_w, kvn_w, wqt_p, wk_p, wvt_p, vone)


def _attention_kernel(qt_ref, qn_ref, k_ref, vt_ref, o_ref, acc_ref, s_ref, bm_ref, *, tq, tk, seq):
    heads = [slice(e * LANE, (e + 1) * LANE) for e in range(2)]
    vheads = [slice(e * MLA_VT_ROWS, (e + 1) * MLA_VT_ROWS) for e in range(2)]
    n_trips = seq // (ATT_SLOTS * tk)

    def scores(slot, j, q_ref):
        off = pl.multiple_of(j * tk, tk)
        block_max = []
        for e in range(2):
            st = jnp.dot(k_ref[0, pl.ds(off, tk), heads[e]], q_ref[0, heads[e], :],
                         preferred_element_type=F32)
            s_ref[slot, e] = st
            block_max.append(jnp.max(st, axis=0, keepdims=True))
        return tuple(block_max)

    def softmax_pv(slot, j, ms, block_max):
        off = pl.multiple_of(j * tk, tk)
        new_ms = []
        for e in range(2):
            m_new = jnp.maximum(ms[e], block_max[e])
            alpha = jnp.exp2(ms[e] - m_new)
            pt = jnp.exp2(s_ref[slot, e] - m_new).astype(BF16)
            pv = jnp.dot(vt_ref[0, vheads[e], pl.ds(off, tk)], pt, preferred_element_type=F32)
            acc_ref[e] = alpha * acc_ref[e] + pv
            new_ms.append(m_new)
        return tuple(new_ms)

    def trip(t, carry, last):
        ms, pending = carry
        j0 = ATT_SLOTS * t
        for u in range(ATT_SLOTS):
            ahead = u + ATT_AHEAD
            if last and ahead >= ATT_SLOTS:
                issued = scores(ahead % ATT_SLOTS, ahead - ATT_SLOTS, qn_ref)
            else:
                issued = scores(ahead % ATT_SLOTS, j0 + ahead, qt_ref)
            ms = softmax_pv(u, j0 + u, ms, pending[0])
            pending = pending[1:] + (issued,)
        return ms, pending

    @pl.when(pl.program_id(2) == 0)
    def _():
        for a in range(ATT_AHEAD):
            first = scores(a, a, qt_ref)
            for e in range(2):
                bm_ref[a, e] = first[e]

    acc_ref[...] = jnp.zeros_like(acc_ref)
    m0 = jnp.full((1, tq), -jnp.inf, F32)
    carry = ((m0, m0), tuple((bm_ref[a, 0], bm_ref[a, 1]) for a in range(ATT_AHEAD)))
    carry = lax.fori_loop(0, n_trips - 1, lambda t, c: trip(t, c, False), carry)
    _, pending = trip(n_trips - 1, carry, True)
    for a in range(ATT_AHEAD):
        for e in range(2):
            bm_ref[a, e] = pending[a][e]
    outs = []
    for e in range(2):
        acc = acc_ref[e]
        outs.append(acc[0:MLA_V] / acc[MLA_V:MLA_V + 1])
    o_ref[0] = jnp.concatenate(outs, axis=0).T.astype(BF16)


def _attention(qt, k, vt, tq, tk):
    B, S, _ = k.shape
    pw = 2 * LANE
    nq = S // tq
    return pl.pallas_call(
        functools.partial(_attention_kernel, tq=tq, tk=tk, seq=S),
        grid=(B, MLA_PAIRS, nq),
        in_specs=[pl.BlockSpec((1, pw, tq), lambda b, p, i: (b, p, i)),
                  pl.BlockSpec((1, pw, tq), lambda b, p, i: (b, p, jnp.minimum(i + 1, nq - 1))),
                  pl.BlockSpec((1, S, pw), lambda b, p, i: (b, 0, p)),
                  pl.BlockSpec((1, 2 * MLA_VT_ROWS, S), lambda b, p, i: (b, p, 0))],
        out_specs=pl.BlockSpec((1, tq, LANE), lambda b, p, i: (b, i, p)),
        out_shape=jax.ShapeDtypeStruct((B, S, MLA_W), BF16),
        scratch_shapes=[pltpu.VMEM((2, MLA_VT_ROWS, tq), F32), pltpu.VMEM((ATT_SLOTS, 2, tk, tq), F32),
                        pltpu.VMEM((ATT_AHEAD, 2, 1, tq), F32)],
        compiler_params=_params("arbitrary", "arbitrary", "arbitrary"),
    )(qt, qt, k, vt)


def _conv_prepare(prev_ref, cur_ref, next_ref, ext_ref, sh_ref, first, last, tm):
    def glu(z):
        return z[:, :CONV_CH] * _sigmoid(z[:, CONV_CH:])

    ext_ref[0:CONV_HALO] = jnp.where(first, 0.0, glu(prev_ref[0]))
    ext_ref[CONV_HALO:CONV_HALO + tm] = glu(cur_ref[0])
    ext_ref[CONV_HALO + tm:2 * CONV_HALO + tm] = jnp.where(last, 0.0, glu(next_ref[0]))
    span = tm + 2 * CONV_HALO - SUBLANE
    for j in range(SUBLANE - 1):
        sh_ref[j] = ext_ref[pl.ds(j + 1, span), :]


def _conv_rows(row_starts, rows, ext_ref, sh_ref, dww_ref, dwb_ref, lnw_ref, lnb_ref, pww_ref, pwb_ref, yc_ref):
    base = CONV_HALO - CONV_K // 2
    for r in row_starts:
        acc = jnp.zeros((rows, CONV_CH), F32) + dwb_ref[...]
        for t in range(CONV_K):
            a, j = divmod(base + t, SUBLANE)
            src = ext_ref[pl.ds(r + a * SUBLANE, rows), :] if j == 0 else sh_ref[j - 1, pl.ds(r + a * SUBLANE, rows), :]
            acc = acc + dww_ref[t:t + 1, :] * src
        mu = jnp.mean(acc, axis=-1, keepdims=True)
        d = acc - mu
        var = jnp.mean(d * d, axis=-1, keepdims=True)
        c = d * lax.rsqrt(var + EPS) * lnw_ref[...] + lnb_ref[...]
        c = (c * _sigmoid(c)).astype(BF16)
        y = jnp.dot(c, pww_ref[...], preferred_element_type=F32) + pwb_ref[...]
        yc_ref[r:r + rows, :] = y.astype(BF16)


def _mixer_tail_kernel(x_ref, q_ref, k_ref, v_ref, g_ref, ub_ref, ym_ref, prev_ref, cur_ref, next_ref,
                       gnw_ref, dcomb_ref, qdf_ref, qdb_ref, kdf_ref, cdf_ref, bm_ref,
                       dww_ref, dwb_ref, lnw_ref, lnb_ref, pww_ref, pwb_ref, wo_ref,
                       o_ref, sf_ref, ext_ref, sh_ref, y_ref, *, tm, n_tiles):
    i = pl.program_id(1)

    @pl.when(i == 0)
    def _():
        sf_ref[...] = jnp.zeros_like(sf_ref)

    _conv_prepare(prev_ref, cur_ref, next_ref, ext_ref, sh_ref, i == 0, i == n_tiles - 1, tm)
    _retention_forward_tile(q_ref, k_ref, v_ref, g_ref, ub_ref, gnw_ref, dcomb_ref, qdf_ref, qdb_ref, kdf_ref,
                            cdf_ref, bm_ref, sf_ref, y_ref, range(tm // RET_CHUNK))
    y_ref[:, RET_W:RET_W + MLA_W] = ym_ref[0]
    _conv_rows(range(0, tm, CONV_ROWS), CONV_ROWS, ext_ref, sh_ref, dww_ref, dwb_ref, lnw_ref, lnb_ref, pww_ref,
               pwb_ref, y_ref.at[:, RET_W + MLA_W:])
    o_ref[0] = x_ref[0] + jnp.dot(y_ref[...], wo_ref[...], preferred_element_type=F32)


def _mixer_tail(x, q, k, v, g, ub, y_mla, cin, gn_w, tables, dw_w, dw_b, ln_w, ln_b, pw_w, pw_b, wo, tm):
    B, S, _ = x.shape
    n_tiles = S // tm
    chunks = tm // RET_CHUNK
    hb = tm // CONV_HALO
    last = S // CONV_HALO - 1
    cw = 2 * CONV_CH

    def tok(width):
        return pl.BlockSpec((1, tm, width), lambda b, i: (b, i, 0))

    row = _const_spec((1, CONV_CH))
    ret_tabs = [tables[n] for n in ("dcomb", "qdf", "qdb", "kdf", "cdf", "bm")]
    return pl.pallas_call(
        functools.partial(_mixer_tail_kernel, tm=tm, n_tiles=n_tiles),
        grid=(B, n_tiles),
        in_specs=[tok(D_MODEL), tok(RET_W), tok(RET_W), tok(RET_W), tok(RET_W),
                  pl.BlockSpec((1, chunks, RET_PAIRS, LANE, LANE), lambda b, i: (b, i, 0, 0, 0)),
                  tok(MLA_W),
                  pl.BlockSpec((1, CONV_HALO, cw), lambda b, i: (b, jnp.maximum(i * hb - 1, 0), 0)),
                  tok(cw),
                  pl.BlockSpec((1, CONV_HALO, cw), lambda b, i: (b, jnp.minimum((i + 1) * hb, last), 0)),
                  _const_spec((1, RET_W))] + [_const_spec(t.shape) for t in ret_tabs] + [
                  _const_spec((CONV_K, CONV_CH)), row, row, row, _const_spec((CONV_CH, CONV_CH)), row,
                  _const_spec((D_MODEL, D_MODEL))],
        out_specs=tok(D_MODEL),
        out_shape=jax.ShapeDtypeStruct((B, S, D_MODEL), F32),
        scratch_shapes=[pltpu.VMEM((RET_PAIRS, LANE, LANE), F32),
                        pltpu.VMEM((tm + 2 * CONV_HALO, CONV_CH), F32),
                        pltpu.VMEM((SUBLANE - 1, tm + 2 * CONV_HALO - SUBLANE, CONV_CH), F32),
                        pltpu.VMEM((tm, D_MODEL), BF16)],
        compiler_params=_params("parallel", "arbitrary"),
    )(x, q, k, v, g, ub, y_mla, cin, cin, cin, gn_w, *ret_tabs, dw_w, dw_b, ln_w, ln_b, pw_w, pw_b, wo)


def _ffn_kernel(prev_ref, cur_ref, next_ref, nw_ref, wg_ref, wu_ref, cwg_ref, cwu_ref, cbg_ref, cbu_ref, wd_ref,
                fw_ref, o_ref, h_ref, act_ref, *, tm, n_tiles, final_norm):
    i = pl.program_id(1)
    nw = nw_ref[...]
    h_ref[0:FFN_HALO] = jnp.where(i > 0, _rms(prev_ref[0], nw), 0.0).astype(BF16)
    h_ref[FFN_HALO:FFN_HALO + tm] = _rms(cur_ref[0], nw).astype(BF16)
    h_ref[FFN_HALO + tm:2 * FFN_HALO + tm] = jnp.where(i < n_tiles - 1, _rms(next_ref[0], nw), 0.0).astype(BF16)

    def conv3(u, w_ref, b_ref, sl):
        w = w_ref[:, sl]
        return (w[0:1] * u[FFN_HALO - 1:FFN_HALO - 1 + tm] + w[1:2] * u[FFN_HALO:FFN_HALO + tm]
                + w[2:3] * u[FFN_HALO + 1:FFN_HALO + 1 + tm] + b_ref[:, sl])

    n_chunks = D_FF // FF_CHUNK
    acc = cur_ref[0]
    group_start = 0
    for c in range(n_chunks):
        sl = slice(c * FF_CHUNK, (c + 1) * FF_CHUNK)
        h = h_ref[...]
        gate = conv3(jnp.dot(h, wg_ref[:, sl], preferred_element_type=F32), cwg_ref, cbg_ref, sl)
        up = conv3(jnp.dot(h, wu_ref[:, sl], preferred_element_type=F32), cwu_ref, cbu_ref, sl)
        act_ref[:, sl] = (gate * _sigmoid(gate) * up).astype(BF16)
        if (c + 1) in FFN_DOWN_BOUNDS:
            ksl = slice(group_start * FF_CHUNK, (c + 1) * FF_CHUNK)
            acc = acc + jnp.dot(act_ref[:, ksl], wd_ref[ksl, :], preferred_element_type=F32)
            group_start = c + 1
    if final_norm:
        acc = _rms(acc, fw_ref[...])
    o_ref[0] = acc


def _ffn(x, norm_w, wg, wu, cwg, cwu, cbg, cbu, wd, final_w, tm, final_norm):
    B, S, _ = x.shape
    n_tiles = S // tm
    hb = tm // FFN_HALO
    last = S // FFN_HALO - 1
    single = pl.Buffered(1)

    def resident(shape):
        return pl.BlockSpec(shape, lambda b, i: (0,) * len(shape), pipeline_mode=single)

    return pl.pallas_call(
        functools.partial(_ffn_kernel, tm=tm, n_tiles=n_tiles, final_norm=final_norm),
        grid=(B, n_tiles),
        in_specs=[pl.BlockSpec((1, FFN_HALO, D_MODEL), lambda b, i: (b, jnp.maximum(i * hb - 1, 0), 0)),
                  pl.BlockSpec((1, tm, D_MODEL), lambda b, i: (b, i, 0)),
                  pl.BlockSpec((1, FFN_HALO, D_MODEL), lambda b, i: (b, jnp.minimum((i + 1) * hb, last), 0)),
                  _const_spec((1, D_MODEL)),
                  resident((D_MODEL, D_FF)), resident((D_MODEL, D_FF)),
                  _const_spec((3, D_FF)), _const_spec((3, D_FF)), _const_spec((1, D_FF)), _const_spec((1, D_FF)),
                  resident((D_FF, D_MODEL)), _const_spec((1, D_MODEL))],
        out_specs=pl.BlockSpec((1, tm, D_MODEL), lambda b, i: (b, i, 0)),
        out_shape=jax.ShapeDtypeStruct((B, S, D_MODEL), F32),
        scratch_shapes=[pltpu.VMEM((tm + 2 * FFN_HALO, D_MODEL), BF16), pltpu.VMEM((tm, D_FF), BF16)],
        compiler_params=_params("parallel", "parallel"),
    )(x, x, x, norm_w, wg, wu, cwg, cwu, cbg, cbu, wd, final_w)


def _prep_layer_weights(w_in, mla_w_uq, mla_w_ukv):
    L = w_in.shape[0]
    c0 = 4 * RET_W
    c1 = c0 + MLA_Q_LORA
    c2 = c1 + MLA_KV_LORA
    c3 = c2 + MLA_ROPE
    z = lambda n: jnp.zeros((L, D_MODEL, n), w_in.dtype)
    w_in_p = jnp.concatenate([w_in[..., :c2], z(MLA_NOPE), w_in[..., c2:c3], z(LANE - MLA_NOPE - MLA_ROPE),
                              w_in[..., c3:]], axis=-1).astype(BF16)

    uq = mla_w_uq.reshape(L, MLA_Q_LORA, MLA_HEADS, MLA_NOPE + MLA_ROPE)
    wq_p = jnp.concatenate([uq, jnp.zeros((L, MLA_Q_LORA, MLA_HEADS, LANE - MLA_NOPE - MLA_ROPE), uq.dtype)],
                           axis=-1).reshape(L, MLA_Q_LORA, MLA_QK_W)
    wqt_p = jnp.swapaxes(wq_p, 1, 2).astype(BF16)

    ukv = mla_w_ukv.reshape(L, MLA_KV_LORA, MLA_HEADS, MLA_NOPE + MLA_V)
    kn = ukv[..., :MLA_NOPE]
    vm = ukv[..., MLA_NOPE:]
    wk_p = jnp.concatenate([kn, jnp.zeros_like(kn)], axis=-1).reshape(L, MLA_KV_LORA, MLA_QK_W).astype(BF16)
    v_pad = jnp.zeros((L, MLA_KV_LORA, MLA_HEADS, MLA_VT_ROWS - MLA_V), vm.dtype)
    wv_p = jnp.concatenate([vm, v_pad], axis=-1).reshape(L, MLA_KV_LORA, MLA_VT_W)
    wvt_p = jnp.swapaxes(wv_p, 1, 2).astype(BF16)

    vone = np.zeros((MLA_HEADS, MLA_VT_ROWS, 1), np.float32)
    vone[:, MLA_V, 0] = 1.0
    return w_in_p, wqt_p, wk_p, wvt_p, jnp.asarray(vone.reshape(MLA_VT_W, 1))


def _tile(S, want):
    return min(S, want)


def kernel(x_prompt, x_sample, attn_norm_w, w_in, ret_gn_w, mla_q_norm_w, mla_w_uq, mla_kv_norm_w, mla_w_ukv,
           conv_dw_w, conv_dw_b, conv_ln_w, conv_ln_b, conv_pw_w, conv_pw_b, w_out, ffn_norm_w, w_up, ffn_conv_w,
           ffn_conv_b, w_down, final_norm_w):
    depth = w_in.shape[0]
    w_in_p, wqt_p, wk_p, wvt_p, vone = _prep_layer_weights(w_in, mla_w_uq, mla_w_ukv)
    w_out_b = w_out.astype(BF16)
    w_gate_b = w_up[..., :D_FF].astype(BF16)
    w_upp_b = w_up[..., D_FF:].astype(BF16)
    w_down_b = w_down.astype(BF16)
    conv_pw_b16 = conv_pw_w.astype(BF16)
    ret_tables = _retention_tables()
    row = lambda a: a[None, :]

    def trunk(x):
        S = x.shape[1]
        cos_r, sin_r, cos_m, sin_m, cos_t, sin_t = _rope_tables(S)
        tm = _tile(S, 512)
        for l in range(depth):
            q, k, v, g, cq, ckv, krp, cin = _inproj(x, row(attn_norm_w[l]), w_in_p[l], cos_r, sin_r, tm)
            ub = _retention_bwd_states(k, v, ret_tables, tm)
            qh, kh, vh = _mla_prep(cq, ckv, krp, cos_m, sin_m, cos_t, sin_t, row(mla_q_norm_w[l]),
                                   row(mla_kv_norm_w[l]), wqt_p[l], wk_p[l], wvt_p[l], vone, tm)
            y_mla = _attention(qh, kh, vh, _tile(S, 512), min(_tile(S, 512), S // ATT_SLOTS))
            x = _mixer_tail(x, q, k, v, g, ub, y_mla, cin, row(ret_gn_w[l]), ret_tables, conv_dw_w[l],
                            row(conv_dw_b[l]), row(conv_ln_w[l]), row(conv_ln_b[l]), conv_pw_b16[l],
                            row(conv_pw_b[l]), w_out_b[l], tm)
            x = _ffn(x, row(ffn_norm_w[l]), w_gate_b[l], w_upp_b[l], ffn_conv_w[l][:, :D_FF], ffn_conv_w[l][:, D_FF:],
                     row(ffn_conv_b[l][:D_FF]), row(ffn_conv_b[l][D_FF:]), w_down_b[l], row(final_norm_w), tm,
                     final_norm=(l == depth - 1))
        return x

    return trunk(x_prompt), trunk(x_sample)
```

```python
import functools

import numpy as np
import jax
import jax.numpy as jnp
from jax import lax
from jax.experimental import pallas as pl
from jax.experimental.pallas import tpu as pltpu

F32 = jnp.float32
BF16 = jnp.bfloat16

LANE = 128
SUBLANE = 8
VMEM_LIMIT = 56 * 1024 * 1024

D_MODEL = 1024
RET_HEADS = 6
RET_HEAD_DIM = 64
RET_W = RET_HEADS * RET_HEAD_DIM
RET_CHUNK = 128
RET_PAIRS = RET_HEADS // 2
MLA_HEADS = 6
MLA_NOPE = 64
MLA_ROPE = 32
MLA_V = 64
MLA_Q_LORA = 384
MLA_KV_LORA = 128
MLA_W = MLA_HEADS * MLA_V
MLA_PAIRS = MLA_HEADS // 2
MLA_QK_W = MLA_HEADS * LANE
MLA_VT_ROWS = 80
MLA_VT_W = MLA_HEADS * MLA_VT_ROWS
CONV_CH = D_MODEL - RET_W - MLA_W
CONV_K = 31
CONV_HALO = 16
D_FF = 2816
FF_CHUNK = 256
FFN_HALO = 16
ATT_AHEAD = 2
ATT_SLOTS = 4
CONV_ROWS = 64
ROPE_BASE = 10000.0
EPS = 1e-6
LOG2_E = float(np.log2(np.e))

C_Q = 0
C_K = C_Q + RET_W
C_V = C_K + RET_W
C_G = C_V + RET_W
C_CQ = C_G + RET_W
C_CKV = C_CQ + MLA_Q_LORA
C_KR = C_CKV + MLA_KV_LORA
C_CONV = C_KR + LANE
IN_COLS_P = C_CONV + 2 * CONV_CH

NT_DIMS = (((1,), (1,)), ((), ()))
TN_DIMS = (((0,), (0,)), ((), ()))


def _params(*sem):
    return pltpu.CompilerParams(dimension_semantics=sem, vmem_limit_bytes=VMEM_LIMIT)


def _const_spec(shape):
    nd = len(shape)
    return pl.BlockSpec(shape, lambda *_: (0,) * nd)


def _rms(x, w):
    ms = jnp.mean(x * x, axis=-1, keepdims=True)
    return x * lax.rsqrt(ms + EPS) * w


def _sigmoid(x):
    return 1.0 / (1.0 + jnp.exp2(x * (-LOG2_E)))


def _rope_table_kernel(fr_ref, sr_ref, fm_ref, sm_ref, fc_ref, cr_ref, snr_ref, cm_ref, snm_ref, ct_ref, snt_ref, *, ts):
    pos = (pl.program_id(0) * ts + lax.broadcasted_iota(jnp.int32, (ts, LANE), 0)).astype(F32)
    ar = pos * fr_ref[...]
    cr_ref[...] = jnp.cos(ar)
    snr_ref[...] = jnp.sin(ar) * sr_ref[...]
    am = pos * fm_ref[...]
    cm_ref[...] = jnp.cos(am)
    snm_ref[...] = jnp.sin(am) * sm_ref[...]
    half_m = MLA_ROPE // 2
    pos_t = (pl.program_id(0) * ts + lax.broadcasted_iota(jnp.int32, (half_m, ts), 1)).astype(F32)
    at = fc_ref[...] * pos_t
    ct_ref[...] = jnp.cos(at)
    snt_ref[...] = jnp.sin(at)


def _rope_tables(S):
    c = np.arange(LANE)
    half_r = RET_HEAD_DIM // 2
    half_m = MLA_ROPE // 2
    freqs_r = 1.0 / (ROPE_BASE ** (jnp.arange(half_r, dtype=F32) / half_r))
    freqs_m = 1.0 / (ROPE_BASE ** (jnp.arange(half_m, dtype=F32) / half_m))
    fr = freqs_r[c % half_r][None, :]
    sr = jnp.asarray(np.where((c % RET_HEAD_DIM) < half_r, -1.0, 1.0), F32)[None, :]
    in_rope = (c >= MLA_NOPE) & (c < MLA_NOPE + MLA_ROPE)
    fm = jnp.where(in_rope, freqs_m[(c - MLA_NOPE) % half_m], 0.0)[None, :]
    sm = jnp.asarray(np.where(c < MLA_NOPE + half_m, -1.0, 1.0), F32)[None, :]
    fc = freqs_m[:, None]
    ts = min(S, 1024)
    row = _const_spec((1, LANE))
    tab = pl.BlockSpec((ts, LANE), lambda i: (i, 0))
    tab_t = pl.BlockSpec((half_m, ts), lambda i: (0, i))
    return pl.pallas_call(
        functools.partial(_rope_table_kernel, ts=ts),
        grid=(S // ts,),
        in_specs=[row, row, row, row, _const_spec((half_m, 1))],
        out_specs=[tab, tab, tab, tab, tab_t, tab_t],
        out_shape=[jax.ShapeDtypeStruct((S, LANE), F32)] * 4 + [jax.ShapeDtypeStruct((half_m, S), F32)] * 2,
        compiler_params=_params("parallel"),
    )(fr, sr, fm, sm, fc)


def _inproj_tile(x_ref, nw_ref, w_ref, cr_ref, sr_ref, q_ref, k_ref, v_ref, g_ref, cin_ref, tm):
    h = _rms(x_ref[0], nw_ref[...]).astype(BF16)

    def seg(lo, width):
        return jnp.dot(h, w_ref[:, lo:lo + width], preferred_element_type=F32)

    cr = cr_ref[...]
    sr = sr_ref[...]
    lane = lax.broadcasted_iota(jnp.int32, (tm, LANE), 1)
    first_half = (lane % RET_HEAD_DIM) < (RET_HEAD_DIM // 2)

    def rope_block(xb):
        partner = jnp.where(first_half, pltpu.roll(xb, LANE - 32, 1), pltpu.roll(xb, 32, 1))
        return xb * cr + partner * sr

    qk = seg(C_Q, 2 * RET_W)
    for b in range(RET_W // LANE):
        sl = slice(b * LANE, (b + 1) * LANE)
        q_ref[0, :, sl] = rope_block(qk[:, sl]).astype(BF16)
        k_ref[0, :, sl] = (rope_block(qk[:, RET_W + b * LANE:RET_W + (b + 1) * LANE])
                           * (RET_HEAD_DIM ** -0.5)).astype(BF16)
    vg = seg(C_V, 2 * RET_W)
    v_ref[0] = vg[:, :RET_W].astype(BF16)
    g_ref[0] = vg[:, RET_W:]
    lat = seg(C_CQ, MLA_Q_LORA + MLA_KV_LORA)
    rest = seg(C_KR, LANE + 2 * CONV_CH)
    cin_ref[0] = rest[:, LANE:]
    return lat[:, :MLA_Q_LORA], lat[:, MLA_Q_LORA:], rest[:, :LANE]


def _retention_tables():
    C = RET_CHUNK
    hh = np.arange(RET_HEADS, dtype=np.float64)
    gf = 1.0 - 2.0 ** (-5.0 - hh)
    gb = 1.0 - 2.0 ** (-5.5 - hh)
    idx = np.arange(C, dtype=np.float64)
    diff = idx[:, None] - idx[None, :]
    dcomb = np.zeros((RET_HEADS, C, C))
    for h in range(RET_HEADS):
        dcomb[h] = np.where(diff >= 0, gf[h] ** np.maximum(diff, 0), gb[h] ** np.maximum(-diff, 0))
    head_of_lane = np.arange(LANE) // RET_HEAD_DIM

    def per_pair_rows(fn):
        out = np.zeros((RET_PAIRS, C, LANE))
        for p in range(RET_PAIRS):
            for e in range(2):
                out[p][:, head_of_lane == e] = fn(2 * p + e)[:, None]
        return out

    qdf = per_pair_rows(lambda h: gf[h] ** (idx + 1.0))
    qdb = per_pair_rows(lambda h: gb[h] ** (C - idx))
    kdf = per_pair_rows(lambda h: gf[h] ** (C - 1.0 - idx))
    kdb = per_pair_rows(lambda h: gb[h] ** idx)
    same_head = head_of_lane[:, None] == head_of_lane[None, :]
    cdf = np.zeros((RET_PAIRS, LANE, LANE))
    cdb = np.zeros((RET_PAIRS, LANE, LANE))
    for p in range(RET_PAIRS):
        for e in range(2):
            blk = (head_of_lane[:, None] == e) & (head_of_lane[None, :] == e)
            cdf[p][blk] = gf[2 * p + e] ** C
            cdb[p][blk] = gb[2 * p + e] ** C
    arrs = dict(dcomb=dcomb, qdf=qdf, qdb=qdb, kdf=kdf, kdb=kdb, cdf=cdf, cdb=cdb, bm=same_head.astype(np.float64))
    return {name: jnp.asarray(a, F32) for name, a in arrs.items()}


def _kv_outer(kp, vp, kdecay, block_mask):
    kd = (kp.astype(F32) * kdecay).astype(BF16)
    return lax.dot_general(kd, vp, TN_DIMS, preferred_element_type=F32) * block_mask


def _retention_bwd_tile(k_ref, v_ref, kdb_ref, cdb_ref, bm_ref, ub_ref, sb_ref, chunks):
    C = RET_CHUNK
    for cc in reversed(range(chunks)):
        rows = slice(cc * C, (cc + 1) * C)
        for p in range(RET_PAIRS):
            cols = slice(p * LANE, (p + 1) * LANE)
            state = sb_ref[p]
            ub_ref[0, cc, p] = state.astype(BF16)
            sb_ref[p] = state * cdb_ref[p] + _kv_outer(k_ref[0, rows, cols], v_ref[0, rows, cols], kdb_ref[p],
                                                       bm_ref[...])


def _retention_forward_tile(q_ref, k_ref, v_ref, g_ref, ub_ref, gnw_ref, dcomb_ref, qdf_ref, qdb_ref, kdf_ref,
                            cdf_ref, bm_ref, sf_ref, yr_ref, chunk_range):
    C = RET_CHUNK
    lane = lax.broadcasted_iota(jnp.int32, (C, LANE), 1)
    low = lane < RET_HEAD_DIM
    inv = 1.0 / RET_HEAD_DIM
    for cc in chunk_range:
        rows = slice(cc * C, (cc + 1) * C)
        for p in range(RET_PAIRS):
            cols = slice(p * LANE, (p + 1) * LANE)
            qp = q_ref[0, rows, cols]
            kp = k_ref[0, rows, cols]
            vp = v_ref[0, rows, cols]
            halves = []
            for e in range(2):
                qe = jnp.where(low if e == 0 else jnp.logical_not(low), qp, jnp.zeros_like(qp))
                s = lax.dot_general(qe, kp, NT_DIMS, preferred_element_type=F32)
                pe = (s * dcomb_ref[2 * p + e]).astype(BF16)
                halves.append(jnp.dot(pe, vp, preferred_element_type=F32))
            y = jnp.where(low, halves[0], halves[1])
            state = sf_ref[p]
            y = y + jnp.dot(qp, state.astype(BF16), preferred_element_type=F32) * qdf_ref[p]
            y = y + jnp.dot(qp, ub_ref[0, cc, p], preferred_element_type=F32) * qdb_ref[p]
            sf_ref[p] = state * cdf_ref[p] + _kv_outer(kp, vp, kdf_ref[p], bm_ref[...])

            mu_lo = jnp.sum(jnp.where(low, y, 0.0), axis=-1, keepdims=True) * inv
            mu_hi = jnp.sum(jnp.where(low, 0.0, y), axis=-1, keepdims=True) * inv
            d = y - jnp.where(low, mu_lo, mu_hi)
            d2 = d * d
            var_lo = jnp.sum(jnp.where(low, d2, 0.0), axis=-1, keepdims=True) * inv
            var_hi = jnp.sum(jnp.where(low, 0.0, d2), axis=-1, keepdims=True) * inv
            yn = d * lax.rsqrt(jnp.where(low, var_lo, var_hi) + EPS) * gnw_ref[:, cols]
            g = g_ref[0, rows, cols]
            yr_ref[rows, cols] = (g * _sigmoid(g) * yn).astype(BF16)


def _mla_prep_tile(cq, ckv, krp, cm_ref, sm_ref, ct_ref, st_ref, qnw_ref, kvnw_ref,
                   wqt_ref, wk_ref, wvt_ref, vone_ref, qt_ref, ko_ref, vt_ref, tm):
    cm = cm_ref[...]
    sm = sm_ref[...]
    lane = lax.broadcasted_iota(jnp.int32, (tm, LANE), 1)
    first_half = lane < MLA_NOPE + MLA_ROPE // 2
    partner_of = lambda xb: jnp.where(first_half, pltpu.roll(xb, LANE - 16, 1), pltpu.roll(xb, 16, 1))

    cqn = _rms(cq, qnw_ref[...]).astype(BF16)
    ckvn = _rms(ckv, kvnw_ref[...]).astype(BF16)
    kr = krp * cm + partner_of(krp) * sm

    ct = ct_ref[...]
    st = st_ref[...]
    qscale = (MLA_NOPE + MLA_ROPE) ** -0.5 * LOG2_E
    half = MLA_ROPE // 2
    r0 = MLA_NOPE
    for h in range(MLA_HEADS):
        base = h * LANE
        rows = slice(base, base + LANE)
        qt = lax.dot_general(wqt_ref[rows, :], cqn, NT_DIMS, preferred_element_type=F32)
        x1 = qt[r0:r0 + half]
        x2 = qt[r0 + half:r0 + 2 * half]
        qt_ref[0, base:base + r0, :] = (qt[0:r0] * qscale).astype(BF16)
        qt_ref[0, base + r0:base + r0 + half, :] = ((x1 * ct - x2 * st) * qscale).astype(BF16)
        qt_ref[0, base + r0 + half:base + r0 + 2 * half, :] = ((x2 * ct + x1 * st) * qscale).astype(BF16)
        qt_ref[0, base + r0 + 2 * half:base + LANE, :] = jnp.zeros((LANE - r0 - 2 * half, tm), BF16)
        ko_ref[0, :, rows] = (jnp.dot(ckvn, wk_ref[:, rows], preferred_element_type=F32) + kr).astype(BF16)
        vrows = slice(h * MLA_VT_ROWS, (h + 1) * MLA_VT_ROWS)
        vt = lax.dot_general(wvt_ref[vrows, :], ckvn, NT_DIMS, preferred_element_type=F32)
        vt_ref[0, vrows, :] = (vt + vone_ref[vrows, :]).astype(BF16)


def _mixer_head_kernel(x_ref, nw_ref, w_ref, cr_ref, sr_ref, cm_ref, sm_ref, ct_ref, st_ref, qnw_ref, kvnw_ref,
                       wqt_ref, wk_ref, wvt_ref, vone_ref, kdb_ref, cdb_ref, bm_ref,
                       q_ref, k_ref, v_ref, g_ref, cin_ref, qt_ref, ko_ref, vt_ref, ub_ref, sb_ref, *, tm):
    @pl.when(pl.program_id(1) == 0)
    def _():
        sb_ref[...] = jnp.zeros_like(sb_ref)

    cq, ckv, krp = _inproj_tile(x_ref, nw_ref, w_ref, cr_ref, sr_ref, q_ref, k_ref, v_ref, g_ref, cin_ref, tm)
    _mla_prep_tile(cq, ckv, krp, cm_ref, sm_ref, ct_ref, st_ref, qnw_ref, kvnw_ref, wqt_ref, wk_ref, wvt_ref,
                   vone_ref, qt_ref, ko_ref, vt_ref, tm)
    _retention_bwd_tile(k_ref, v_ref, kdb_ref, cdb_ref, bm_ref, ub_ref, sb_ref, tm // RET_CHUNK)


def _mixer_head(x, norm_w, w_in_p, cos_r, sin_r, cos_m, sin_m, cos_t, sin_t, qn_w, kvn_w, wqt_p, wk_p, wvt_p, vone,
                tables, tm):
    B, S, _ = x.shape
    T = S // tm
    chunks = tm // RET_CHUNK
    kdb, cdb, bm = tables["kdb"], tables["cdb"], tables["bm"]

    def tok(width):
        return pl.BlockSpec((1, tm, width), lambda b, i: (b, T - 1 - i, 0))

    def feat(rows):
        return pl.BlockSpec((1, rows, tm), lambda b, i: (b, 0, T - 1 - i))

    tab = pl.BlockSpec((tm, LANE), lambda b, i: (T - 1 - i, 0))
    tab_t = pl.BlockSpec((MLA_ROPE // 2, tm), lambda b, i: (0, T - 1 - i))
    ub_spec = pl.BlockSpec((1, chunks, RET_PAIRS, LANE, LANE), lambda b, i: (b, T - 1 - i, 0, 0, 0))
    tok_out = [(RET_W, BF16), (RET_W, BF16), (RET_W, BF16), (RET_W, F32), (2 * CONV_CH, F32)]
    return pl.pallas_call(
        functools.partial(_mixer_head_kernel, tm=tm),
        grid=(B, T),
        in_specs=[tok(D_MODEL), _const_spec((1, D_MODEL)), _const_spec((D_MODEL, IN_COLS_P)), tab, tab, tab, tab,
                  tab_t, tab_t, _const_spec((1, MLA_Q_LORA)), _const_spec((1, MLA_KV_LORA)),
                  _const_spec((MLA_QK_W, MLA_Q_LORA)), _const_spec((MLA_KV_LORA, MLA_QK_W)),
                  _const_spec((MLA_VT_W, MLA_KV_LORA)), _const_spec((MLA_VT_W, 1)),
                  _const_spec(kdb.shape), _const_spec(cdb.shape), _const_spec(bm.shape)],
        out_specs=[tok(w) for w, _ in tok_out] + [feat(MLA_QK_W), tok(MLA_QK_W), feat(MLA_VT_W), ub_spec],
        out_shape=[jax.ShapeDtypeStruct((B, S, w), d) for w, d in tok_out] + [
            jax.ShapeDtypeStruct((B, MLA_QK_W, S), BF16), jax.ShapeDtypeStruct((B, S, MLA_QK_W), BF16),
            jax.ShapeDtypeStruct((B, MLA_VT_W, S), BF16),
            jax.ShapeDtypeStruct((B, S // RET_CHUNK, RET_PAIRS, LANE, LANE), BF16)],
        scratch_shapes=[pltpu.VMEM((RET_PAIRS, LANE, LANE), F32)],
        compiler_params=_params("parallel", "arbitrary"),
    )(x, norm_w, w_in_p, cos_r, sin_r, cos_m, sin_m, cos_t, sin_t, qn_w, kvn_w, wqt_p, wk_p, wvt_p, vone,
      kdb, cdb, bm)


def _attention_kernel(qt_ref, qn_ref, k_ref, vt_ref, o_ref, acc_ref, s_ref, bm_ref, *, tq, tk, seq):
    heads = [slice(e * LANE, (e + 1) * LANE) for e in range(2)]
    vheads = [slice(e * MLA_VT_ROWS, (e + 1) * MLA_VT_ROWS) for e in range(2)]
    n_trips = seq // (ATT_SLOTS * tk)

    def scores(slot, j, q_ref):
        off = pl.multiple_of(j * tk, tk)
        block_max = []
        for e in range(2):
            st = jnp.dot(k_ref[0, pl.ds(off, tk), heads[e]], q_ref[0, heads[e], :],
                         preferred_element_type=F32)
            s_ref[slot, e] = st
            block_max.append(jnp.max(st, axis=0, keepdims=True))
        return tuple(block_max)

    def softmax_pv(slot, j, ms, block_max):
        off = pl.multiple_of(j * tk, tk)
        new_ms = []
        for e in range(2):
            m_new = jnp.maximum(ms[e], block_max[e])
            alpha = jnp.exp2(ms[e] - m_new)
            pt = jnp.exp2(s_ref[slot, e] - m_new).astype(BF16)
            pv = jnp.dot(vt_ref[0, vheads[e], pl.ds(off, tk)], pt, preferred_element_type=F32)
            acc_ref[e] = alpha * acc_ref[e] + pv
            new_ms.append(m_new)
        return tuple(new_ms)

    def trip(t, carry, last):
        ms, pending = carry
        j0 = ATT_SLOTS * t
        for u in range(ATT_SLOTS):
            ahead = u + ATT_AHEAD
            if last and ahead >= ATT_SLOTS:
                issued = scores(ahead % ATT_SLOTS, ahead - ATT_SLOTS, qn_ref)
            else:
                issued = scores(ahead % ATT_SLOTS, j0 + ahead, qt_ref)
            ms = softmax_pv(u, j0 + u, ms, pending[0])
            pending = pending[1:] + (issued,)
        return ms, pending

    @pl.when(pl.program_id(2) == 0)
    def _():
        for a in range(ATT_AHEAD):
            first = scores(a, a, qt_ref)
            for e in range(2):
                bm_ref[a, e] = first[e]

    acc_ref[...] = jnp.zeros_like(acc_ref)
    m0 = jnp.full((1, tq), -jnp.inf, F32)
    carry = ((m0, m0), tuple((bm_ref[a, 0], bm_ref[a, 1]) for a in range(ATT_AHEAD)))
    carry = lax.fori_loop(0, n_trips - 1, lambda t, c: trip(t, c, False), carry)
    _, pending = trip(n_trips - 1, carry, True)
    for a in range(ATT_AHEAD):
        for e in range(2):
            bm_ref[a, e] = pending[a][e]
    outs = []
    for e in range(2):
        acc = acc_ref[e]
        outs.append(acc[0:MLA_V] / acc[MLA_V:MLA_V + 1])
    o_ref[0] = jnp.concatenate(outs, axis=0).T.astype(BF16)


def _attention(qt, k, vt, tq, tk):
    B, S, _ = k.shape
    pw = 2 * LANE
    nq = S // tq
    return pl.pallas_call(
        functools.partial(_attention_kernel, tq=tq, tk=tk, seq=S),
        grid=(B, MLA_PAIRS, nq),
        in_specs=[pl.BlockSpec((1, pw, tq), lambda b, p, i: (b, p, i)),
                  pl.BlockSpec((1, pw, tq), lambda b, p, i: (b, p, jnp.minimum(i + 1, nq - 1))),
                  pl.BlockSpec((1, S, pw), lambda b, p, i: (b, 0, p)),
                  pl.BlockSpec((1, 2 * MLA_VT_ROWS, S), lambda b, p, i: (b, p, 0))],
        out_specs=pl.BlockSpec((1, tq, LANE), lambda b, p, i: (b, i, p)),
        out_shape=jax.ShapeDtypeStruct((B, S, MLA_W), BF16),
        scratch_shapes=[pltpu.VMEM((2, MLA_VT_ROWS, tq), F32), pltpu.VMEM((ATT_SLOTS, 2, tk, tq), F32),
                        pltpu.VMEM((ATT_AHEAD, 2, 1, tq), F32)],
        compiler_params=_params("arbitrary", "arbitrary", "arbitrary"),
    )(qt, qt, k, vt)


def _conv_prepare(prev_ref, cur_ref, next_ref, ext_ref, sh_ref, first, last, tm):
    def glu(z):
        return z[:, :CONV_CH] * _sigmoid(z[:, CONV_CH:])

    ext_ref[0:CONV_HALO] = jnp.where(first, 0.0, glu(prev_ref[0]))
    ext_ref[CONV_HALO:CONV_HALO + tm] = glu(cur_ref[0])
    ext_ref[CONV_HALO + tm:2 * CONV_HALO + tm] = jnp.where(last, 0.0, glu(next_ref[0]))
    span = tm + 2 * CONV_HALO - SUBLANE
    for j in range(SUBLANE - 1):
        sh_ref[j] = ext_ref[pl.ds(j + 1, span), :]


def _conv_rows(row_starts, rows, ext_ref, sh_ref, dww_ref, dwb_ref, lnw_ref, lnb_ref, pww_ref, pwb_ref, yc_ref):
    base = CONV_HALO - CONV_K // 2
    for r in row_starts:
        acc = jnp.zeros((rows, CONV_CH), F32) + dwb_ref[...]
        for t in range(CONV_K):
            a, j = divmod(base + t, SUBLANE)
            src = ext_ref[pl.ds(r + a * SUBLANE, rows), :] if j == 0 else sh_ref[j - 1, pl.ds(r + a * SUBLANE, rows), :]
            acc = acc + dww_ref[t:t + 1, :] * src
        mu = jnp.mean(acc, axis=-1, keepdims=True)
        d = acc - mu
        var = jnp.mean(d * d, axis=-1, keepdims=True)
        c = d * lax.rsqrt(var + EPS) * lnw_ref[...] + lnb_ref[...]
        c = (c * _sigmoid(c)).astype(BF16)
        y = jnp.dot(c, pww_ref[...], preferred_element_type=F32) + pwb_ref[...]
        yc_ref[r:r + rows, :] = y.astype(BF16)


def _mixer_tail_kernel(x_ref, q_ref, k_ref, v_ref, g_ref, ub_ref, ym_ref, prev_ref, cur_ref, next_ref,
                       gnw_ref, dcomb_ref, qdf_ref, qdb_ref, kdf_ref, cdf_ref, bm_ref,
                       dww_ref, dwb_ref, lnw_ref, lnb_ref, pww_ref, pwb_ref, wo_ref,
                       o_ref, sf_ref, ext_ref, sh_ref, y_ref, *, tm, n_tiles):
    i = pl.program_id(1)

    @pl.when(i == 0)
    def _():
        sf_ref[...] = jnp.zeros_like(sf_ref)

    _conv_prepare(prev_ref, cur_ref, next_ref, ext_ref, sh_ref, i == 0, i == n_tiles - 1, tm)
    _retention_forward_tile(q_ref, k_ref, v_ref, g_ref, ub_ref, gnw_ref, dcomb_ref, qdf_ref, qdb_ref, kdf_ref,
                            cdf_ref, bm_ref, sf_ref, y_ref, range(tm // RET_CHUNK))
    y_ref[:, RET_W:RET_W + MLA_W] = ym_ref[0]
    _conv_rows(range(0, tm, CONV_ROWS), CONV_ROWS, ext_ref, sh_ref, dww_ref, dwb_ref, lnw_ref, lnb_ref, pww_ref,
               pwb_ref, y_ref.at[:, RET_W + MLA_W:])
    o_ref[0] = x_ref[0] + jnp.dot(y_ref[...], wo_ref[...], preferred_element_type=F32)


def _mixer_tail(x, q, k, v, g, ub, y_mla, cin, gn_w, tables, dw_w, dw_b, ln_w, ln_b, pw_w, pw_b, wo, tm):
    B, S, _ = x.shape
    n_tiles = S // tm
    chunks = tm // RET_CHUNK
    hb = tm // CONV_HALO
    last = S // CONV_HALO - 1
    cw = 2 * CONV_CH

    def tok(width):
        return pl.BlockSpec((1, tm, width), lambda b, i: (b, i, 0))

    row = _const_spec((1, CONV_CH))
    ret_tabs = [tables[n] for n in ("dcomb", "qdf", "qdb", "kdf", "cdf", "bm")]
    return pl.pallas_call(
        functools.partial(_mixer_tail_kernel, tm=tm, n_tiles=n_tiles),
        grid=(B, n_tiles),
        in_specs=[tok(D_MODEL), tok(RET_W), tok(RET_W), tok(RET_W), tok(RET_W),
                  pl.BlockSpec((1, chunks, RET_PAIRS, LANE, LANE), lambda b, i: (b, i, 0, 0, 0)),
                  tok(MLA_W),
                  pl.BlockSpec((1, CONV_HALO, cw), lambda b, i: (b, jnp.maximum(i * hb - 1, 0), 0)),
                  tok(cw),
                  pl.BlockSpec((1, CONV_HALO, cw), lambda b, i: (b, jnp.minimum((i + 1) * hb, last), 0)),
                  _const_spec((1, RET_W))] + [_const_spec(t.shape) for t in ret_tabs] + [
                  _const_spec((CONV_K, CONV_CH)), row, row, row, _const_spec((CONV_CH, CONV_CH)), row,
                  _const_spec((D_MODEL, D_MODEL))],
        out_specs=tok(D_MODEL),
        out_shape=jax.ShapeDtypeStruct((B, S, D_MODEL), F32),
        scratch_shapes=[pltpu.VMEM((RET_PAIRS, LANE, LANE), F32),
                        pltpu.VMEM((tm + 2 * CONV_HALO, CONV_CH), F32),
                        pltpu.VMEM((SUBLANE - 1, tm + 2 * CONV_HALO - SUBLANE, CONV_CH), F32),
                        pltpu.VMEM((tm, D_MODEL), BF16)],
        compiler_params=_params("parallel", "arbitrary"),
    )(x, q, k, v, g, ub, y_mla, cin, cin, cin, gn_w, *ret_tabs, dw_w, dw_b, ln_w, ln_b, pw_w, pw_b, wo)


def _ffn_kernel(prev_ref, cur_ref, next_ref, nw_ref, wg_ref, wu_ref, cwg_ref, cwu_ref, cbg_ref, cbu_ref, wd_ref,
                fw_ref, o_ref, h_ref, act_ref, *, tm, n_tiles, final_norm):
    i = pl.program_id(1)
    nw = nw_ref[...]
    h_ref[0:FFN_HALO] = jnp.where(i > 0, _rms(prev_ref[0], nw), 0.0).astype(BF16)
    h_ref[FFN_HALO:FFN_HALO + tm] = _rms(cur_ref[0], nw).astype(BF16)
    h_ref[FFN_HALO + tm:2 * FFN_HALO + tm] = jnp.where(i < n_tiles - 1, _rms(next_ref[0], nw), 0.0).astype(BF16)

    def conv3(u, w_ref, b_ref, sl):
        w = w_ref[:, sl]
        return (w[0:1] * u[FFN_HALO - 1:FFN_HALO - 1 + tm] + w[1:2] * u[FFN_HALO:FFN_HALO + tm]
                + w[2:3] * u[FFN_HALO + 1:FFN_HALO + 1 + tm] + b_ref[:, sl])

    for c in range(D_FF // FF_CHUNK):
        sl = slice(c * FF_CHUNK, (c + 1) * FF_CHUNK)
        h = h_ref[...]
        gate = conv3(jnp.dot(h, wg_ref[:, sl], preferred_element_type=F32), cwg_ref, cbg_ref, sl)
        up = conv3(jnp.dot(h, wu_ref[:, sl], preferred_element_type=F32), cwu_ref, cbu_ref, sl)
        act_ref[:, sl] = (gate * _sigmoid(gate) * up).astype(BF16)
    acc = cur_ref[0] + jnp.dot(act_ref[...], wd_ref[...], preferred_element_type=F32)
    if final_norm:
        acc = _rms(acc, fw_ref[...])
    o_ref[0] = acc


def _ffn(x, norm_w, wg, wu, cwg, cwu, cbg, cbu, wd, final_w, tm, final_norm):
    B, S, _ = x.shape
    n_tiles = S // tm
    hb = tm // FFN_HALO
    last = S // FFN_HALO - 1
    single = pl.Buffered(1)

    def resident(shape):
        return pl.BlockSpec(shape, lambda b, i: (0,) * len(shape), pipeline_mode=single)

    return pl.pallas_call(
        functools.partial(_ffn_kernel, tm=tm, n_tiles=n_tiles, final_norm=final_norm),
        grid=(B, n_tiles),
        in_specs=[pl.BlockSpec((1, FFN_HALO, D_MODEL), lambda b, i: (b, jnp.maximum(i * hb - 1, 0), 0)),
                  pl.BlockSpec((1, tm, D_MODEL), lambda b, i: (b, i, 0)),
                  pl.BlockSpec((1, FFN_HALO, D_MODEL), lambda b, i: (b, jnp.minimum((i + 1) * hb, last), 0)),
                  _const_spec((1, D_MODEL)),
                  resident((D_MODEL, D_FF)), resident((D_MODEL, D_FF)),
                  _const_spec((3, D_FF)), _const_spec((3, D_FF)), _const_spec((1, D_FF)), _const_spec((1, D_FF)),
                  resident((D_FF, D_MODEL)), _const_spec((1, D_MODEL))],
        out_specs=pl.BlockSpec((1, tm, D_MODEL), lambda b, i: (b, i, 0)),
        out_shape=jax.ShapeDtypeStruct((B, S, D_MODEL), F32),
        scratch_shapes=[pltpu.VMEM((tm + 2 * FFN_HALO, D_MODEL), BF16), pltpu.VMEM((tm, D_FF), BF16)],
        compiler_params=_params("parallel", "parallel"),
    )(x, x, x, norm_w, wg, wu, cwg, cwu, cbg, cbu, wd, final_w)


def _prep_layer_weights(w_in, mla_w_uq, mla_w_ukv):
    L = w_in.shape[0]
    c0 = 4 * RET_W
    c1 = c0 + MLA_Q_LORA
    c2 = c1 + MLA_KV_LORA
    c3 = c2 + MLA_ROPE
    z = lambda n: jnp.zeros((L, D_MODEL, n), w_in.dtype)
    w_in_p = jnp.concatenate([w_in[..., :c2], z(MLA_NOPE), w_in[..., c2:c3], z(LANE - MLA_NOPE - MLA_ROPE),
                              w_in[..., c3:]], axis=-1).astype(BF16)

    uq = mla_w_uq.reshape(L, MLA_Q_LORA, MLA_HEADS, MLA_NOPE + MLA_ROPE)
    wq_p = jnp.concatenate([uq, jnp.zeros((L, MLA_Q_LORA, MLA_HEADS, LANE - MLA_NOPE - MLA_ROPE), uq.dtype)],
                           axis=-1).reshape(L, MLA_Q_LORA, MLA_QK_W)
    wqt_p = jnp.swapaxes(wq_p, 1, 2).astype(BF16)

    ukv = mla_w_ukv.reshape(L, MLA_KV_LORA, MLA_HEADS, MLA_NOPE + MLA_V)
    kn = ukv[..., :MLA_NOPE]
    vm = ukv[..., MLA_NOPE:]
    wk_p = jnp.concatenate([kn, jnp.zeros_like(kn)], axis=-1).reshape(L, MLA_KV_LORA, MLA_QK_W).astype(BF16)
    v_pad = jnp.zeros((L, MLA_KV_LORA, MLA_HEADS, MLA_VT_ROWS - MLA_V), vm.dtype)
    wv_p = jnp.concatenate([vm, v_pad], axis=-1).reshape(L, MLA_KV_LORA, MLA_VT_W)
    wvt_p = jnp.swapaxes(wv_p, 1, 2).astype(BF16)

    vone = np.zeros((MLA_HEADS, MLA_VT_ROWS, 1), np.float32)
    vone[:, MLA_V, 0] = 1.0
    return w_in_p, wqt_p, wk_p, wvt_p, jnp.asarray(vone.reshape(MLA_VT_W, 1))


def _tile(S, want):
    return min(S, want)


def kernel(x_prompt, x_sample, attn_norm_w, w_in, ret_gn_w, mla_q_norm_w, mla_w_uq, mla_kv_norm_w, mla_w_ukv,
           conv_dw_w, conv_dw_b, conv_ln_w, conv_ln_b, conv_pw_w, conv_pw_b, w_out, ffn_norm_w, w_up, ffn_conv_w,
           ffn_conv_b, w_down, final_norm_w):
    depth = w_in.shape[0]
    w_in_p, wqt_p, wk_p, wvt_p, vone = _prep_layer_weights(w_in, mla_w_uq, mla_w_ukv)
    w_out_b = w_out.astype(BF16)
    w_gate_b = w_up[..., :D_FF].astype(BF16)
    w_upp_b = w_up[..., D_FF:].astype(BF16)
    w_down_b = w_down.astype(BF16)
    conv_pw_b16 = conv_pw_w.astype(BF16)
    ret_tables = _retention_tables()
    row = lambda a: a[None, :]

    def trunk(x):
        S = x.shape[1]
        cos_r, sin_r, cos_m, sin_m, cos_t, sin_t = _rope_tables(S)
        tm = _tile(S, 512)
        for l in range(depth):
            q, k, v, g, cin, qh, kh, vh, ub = _mixer_head(
                x, row(attn_norm_w[l]), w_in_p[l], cos_r, sin_r, cos_m, sin_m, cos_t, sin_t, row(mla_q_norm_w[l]),
                row(mla_kv_norm_w[l]), wqt_p[l], wk_p[l], wvt_p[l], vone, ret_tables, tm)
            y_mla = _attention(qh, kh, vh, _tile(S, 512), min(_tile(S, 512), S // ATT_SLOTS))
            x = _mixer_tail(x, q, k, v, g, ub, y_mla, cin, row(ret_gn_w[l]), ret_tables, conv_dw_w[l],
                            row(conv_dw_b[l]), row(conv_ln_w[l]), row(conv_ln_b[l]), conv_pw_b16[l],
                            row(conv_pw_b[l]), w_out_b[l], tm)
            x = _ffn(x, row(ffn_norm_w[l]), w_gate_b[l], w_upp_b[l], ffn_conv_w[l][:, :D_FF], ffn_conv_w[l][:, D_FF:],
                     row(ffn_conv_b[l][:D_FF]), row(ffn_conv_b[l][D_FF:]), w_down_b[l], row(final_norm_w), tm,
                     final_norm=(l == depth - 1))
        return x

    return trunk(x_prompt), trunk(x_sample)
```
